```python
import math
import jax, jax.numpy as jnp
from jax import lax
import numpy as np

D_MODEL = 2048
BATCH = 1
SEQ = 8192
DEPTH = 2
DEC_BATCH = 128
DEC_SEQ = 1
PAST_LEN = 16384
PAGE_SIZE = 128

N_MIXERS = 2
N_MLA_LAYERS = (DEPTH + 1) // 2
N_CONV_LAYERS = DEPTH // 2
N_HEADS = 16
QK_NOPE_DIM = 128
QK_ROPE_DIM = 64
V_HEAD_DIM = 128
Q_LORA_RANK = 512
KV_LORA_RANK = 512
ROPE_THETA = 10000.0
ATTN_SCALE = (QK_NOPE_DIM + QK_ROPE_DIM) ** -0.5
Q_BLOCK = 128
CONV_WIDTH = 3
N_EXPERTS = 32
TOP_K = 4
D_EXPERT = D_MODEL
SWIGLU_LIMIT = 7.0
SWIGLU_ALPHA = 1.702
MOE_BLOCK = 128
N_ADA = 6
NORM_EPS = 1e-6
NEG_INF = -1e30

kernel_name = 'mla_shortconv_moe_adaln_step'


def rmsnorm(x, g):
    xf = x.astype(jnp.float32)
    y = xf * lax.rsqrt(jnp.mean(xf * xf, axis=-1, keepdims=True) + NORM_EPS)
    return (y * g.astype(jnp.float32)).astype(x.dtype)


def adaln(c, w, b):
    mod = jax.nn.silu(c) @ w + b
    return jnp.split(mod[:, None, :], N_ADA, axis=-1)


def modulate(h, shift, scale):
    return h * (1.0 + scale) + shift


def rope_tables(pos):
    half = QK_ROPE_DIM // 2
    inv = jnp.exp(-math.log(ROPE_THETA) * jnp.arange(half, dtype=jnp.float32) / half)
    ang = pos.astype(jnp.float32)[:, None] * inv[None, :]
    return jnp.cos(ang), jnp.sin(ang)


def apply_rope(x, cos, sin):
    x1, x2 = jnp.split(x.astype(jnp.float32), 2, axis=-1)
    return jnp.concatenate([x1 * cos - x2 * sin, x1 * sin + x2 * cos], axis=-1).astype(x.dtype)


def mla_project(h, pos, w_dq, g_q, w_uq, w_dkv, g_kv):
    b, t, _ = h.shape
    cq = rmsnorm(h @ w_dq, g_q)
    q = (cq @ w_uq).reshape(b, t, N_HEADS, QK_NOPE_DIM + QK_ROPE_DIM)
    kv_a = h @ w_dkv
    ckv = rmsnorm(kv_a[..., :KV_LORA_RANK], g_kv)
    cos, sin = rope_tables(pos)
    q_nope = q[..., :QK_NOPE_DIM]
    q_rope = apply_rope(q[..., QK_NOPE_DIM:], cos[:, None, :], sin[:, None, :])
    k_rope = apply_rope(kv_a[..., KV_LORA_RANK:], cos, sin)
    return q_nope, q_rope, ckv, k_rope


def mla_prompt(h, w_dq, g_q, w_uq, w_dkv, g_kv, w_ukv, w_o):
    b, t, _ = h.shape
    pos = jnp.arange(t, dtype=jnp.int32)
    q_nope, q_rope, ckv, k_rope = mla_project(h, pos, w_dq, g_q, w_uq, w_dkv, g_kv)
    w_ukv_h = w_ukv.reshape(KV_LORA_RANK, N_HEADS, QK_NOPE_DIM + V_HEAD_DIM)
    kv = jnp.einsum('btc,chd->bthd', ckv, w_ukv_h)
    k_nope, v = kv[..., :QK_NOPE_DIM], kv[..., QK_NOPE_DIM:]
    key_pos = jnp.arange(t)

    def q_block(i):
        s0 = i * Q_BLOCK
        qn = lax.dynamic_slice_in_dim(q_nope, s0, Q_BLOCK, axis=1)
        qr = lax.dynamic_slice_in_dim(q_rope, s0, Q_BLOCK, axis=1)
        s = (jnp.einsum('bqhd,bkhd->bhqk', qn, k_nope, preferred_element_type=jnp.float32)
             + jnp.einsum('bqhr,bkr->bhqk', qr, k_rope, preferred_element_type=jnp.float32)) * ATTN_SCALE
        qpos = s0 + jnp.arange(Q_BLOCK)
        s = jnp.where(key_pos[None, :] <= qpos[:, None], s, NEG_INF)
        p = jax.nn.softmax(s, axis=-1).astype(v.dtype)
        return jnp.einsum('bhqk,bkhd->bqhd', p, v)

    o = lax.map(q_block, jnp.arange(t // Q_BLOCK))
    o = jnp.transpose(o, (1, 0, 2, 3, 4)).reshape(b, t, N_HEADS * V_HEAD_DIM)
    return o @ w_o, ckv, k_rope


def mla_sample(h, cache_ckv_l, cache_krope_l, page_table, w_dq, g_q, w_uq, w_dkv, g_kv, w_ukv, w_o):
    bd, t, _ = h.shape
    pos = PAST_LEN + jnp.arange(t, dtype=jnp.int32)
    q_nope, q_rope, ckv, k_rope = mla_project(h, pos, w_dq, g_q, w_uq, w_dkv, g_kv)
    w_ukv_h = w_ukv.reshape(KV_LORA_RANK, N_HEADS, QK_NOPE_DIM + V_HEAD_DIM)
    w_uk, w_uv = w_ukv_h[..., :QK_NOPE_DIM], w_ukv_h[..., QK_NOPE_DIM:]
    q_lat = jnp.einsum('bqhd,chd->bqhc', q_nope, w_uk)

    def scores(ck, kr):
        return (jnp.einsum('bqhc,bkc->bhqk', q_lat, ck, preferred_element_type=jnp.float32)
                + jnp.einsum('bqhr,bkr->bhqk', q_rope, kr, preferred_element_type=jnp.float32)) * ATTN_SCALE

    s_self = scores(ckv, k_rope)
    causal = jnp.arange(t)[None, :] <= jnp.arange(t)[:, None]
    s_self = jnp.where(causal, s_self, NEG_INF)
    m0 = jnp.max(s_self, axis=-1)
    p0 = jnp.exp(s_self - m0[..., None])
    l0 = jnp.sum(p0, axis=-1)
    acc0 = jnp.einsum('bhqk,bkc->bhqc', p0, ckv.astype(jnp.float32))

    def page_step(carry, page_ids):
        m, l, acc = carry
        ck = cache_ckv_l[page_ids]
        kr = cache_krope_l[page_ids]
        s = scores(ck, kr)
        m_new = jnp.maximum(m, jnp.max(s, axis=-1))
        corr = jnp.exp(m - m_new)
        p = jnp.exp(s - m_new[..., None])
        acc = acc * corr[..., None] + jnp.einsum('bhqk,bkc->bhqc', p, ck.astype(jnp.float32))
        return (m_new, l * corr + jnp.sum(p, axis=-1), acc), None

    (m, l, acc), _ = lax.scan(page_step, (m0, l0, acc0), page_table.T)
    o_lat = (acc / l[..., None]).astype(h.dtype)
    o = jnp.einsum('bhqc,chd->bqhd', o_lat, w_uv).reshape(bd, t, N_HEADS * V_HEAD_DIM)
    return o @ w_o, ckv, k_rope


def short_conv(h, conv_state, w_in, w_conv, w_out):
    t = h.shape[1]
    b_gate, c_gate, xt = jnp.split(h @ w_in, 3, axis=-1)
    u = c_gate * xt
    u_pad = jnp.concatenate([conv_state.astype(u.dtype), u], axis=1)
    conv = sum(w_conv[j] * u_pad[:, j:j + t] for j in range(CONV_WIDTH))
    return (b_gate * conv) @ w_out, u_pad[:, t:]


def moe(xt, w_router, b_router, w_gu, b_gu, w_down, b_down):
    n, d = xt.shape
    logits = (xt @ w_router).astype(jnp.float32) + b_router.astype(jnp.float32)
    top_val, top_idx = lax.top_k(logits, TOP_K)
    gates = jax.nn.softmax(top_val, axis=-1)
    n_assign = n * TOP_K
    flat_e = top_idx.reshape(-1)
    flat_tok = jnp.repeat(jnp.arange(n, dtype=jnp.int32), TOP_K)
    flat_g = gates.reshape(-1)
    order = jnp.argsort(flat_e, stable=True)
    e_sorted = flat_e[order]
    counts = jnp.zeros((N_EXPERTS,), jnp.int32).at[flat_e].add(1)
    padded = (counts + MOE_BLOCK - 1) // MOE_BLOCK * MOE_BLOCK
    pad_end = jnp.cumsum(padded)
    pad_start = pad_end - padded
    start = jnp.cumsum(counts) - counts
    dest = pad_start[e_sorted] + (jnp.arange(n_assign, dtype=jnp.int32) - start[e_sorted])
    n_rows = (n_assign + MOE_BLOCK - 1) // MOE_BLOCK * MOE_BLOCK + N_EXPERTS * MOE_BLOCK
    n_blocks = n_rows // MOE_BLOCK
    row_tok = jnp.full((n_rows,), n, jnp.int32).at[dest].set(flat_tok[order])
    row_gate = jnp.zeros((n_rows,), jnp.float32).at[dest].set(flat_g[order])
    blk_expert = jnp.minimum(jnp.searchsorted(pad_end, jnp.arange(n_blocks, dtype=jnp.int32) * MOE_BLOCK, side='right'), N_EXPERTS - 1)
    x_rows = jnp.concatenate([xt, jnp.zeros((1, d), xt.dtype)], axis=0)

    def run_block(args):
        tok, e = args
        xb = x_rows[tok]
        gu = xb @ w_gu[e] + b_gu[e]
        g = jnp.minimum(gu[:, :D_EXPERT], SWIGLU_LIMIT)
        u = jnp.clip(gu[:, D_EXPERT:], -SWIGLU_LIMIT, SWIGLU_LIMIT)
        act = (u + 1.0) * (g * jax.nn.sigmoid(SWIGLU_ALPHA * g))
        return act @ w_down[e] + b_down[e]

    out = lax.map(run_block, (row_tok.reshape(n_blocks, MOE_BLOCK), blk_expert))
    out = out.reshape(n_rows, d) * row_gate[:, None].astype(out.dtype)
    return jax.ops.segment_sum(out, row_tok, num_segments=n + 1)[:n]


def setup_inputs(seed: int = 0) -> dict:
    key = jax.random.key(seed)
    ks = iter(jax.random.split(key, 40))
    n_pages = PAST_LEN // PAGE_SIZE
    n_phys = DEC_BATCH * n_pages * 5 // 4
    f32 = jnp.float32

    def nrm(shape, scale=1.0):
        return jax.random.normal(next(ks), shape, f32) * scale

    def gain(shape):
        return 1.0 + nrm(shape, 0.05)

    page_table = jax.random.permutation(next(ks), n_phys)[:DEC_BATCH * n_pages].reshape(DEC_BATCH, n_pages).astype(jnp.int32)
    d = D_MODEL
    return {
        'x_prompt': nrm((BATCH, SEQ, d)),
        'x_sample': nrm((DEC_BATCH, DEC_SEQ, d)),
        'c_prompt': nrm((BATCH, d)),
        'c_sample': nrm((DEC_BATCH, d)),
        'cache_ckv': nrm((N_MLA_LAYERS, n_phys, PAGE_SIZE, KV_LORA_RANK)),
        'cache_krope': nrm((N_MLA_LAYERS, n_phys, PAGE_SIZE, QK_ROPE_DIM)),
        'state_conv': nrm((N_CONV_LAYERS, DEC_BATCH, CONV_WIDTH - 1, d)),
        'page_table': page_table,
        'w_ada': nrm((DEPTH, d, N_ADA * d), 0.5 * d ** -0.5),
        'b_ada': nrm((DEPTH, N_ADA * d), 0.02),
        'g_mix_norm': gain((DEPTH, d)),
        'g_ffn_norm': gain((DEPTH, d)),
        'w_dq': nrm((N_MLA_LAYERS, d, Q_LORA_RANK), d ** -0.5),
        'g_q': gain((N_MLA_LAYERS, Q_LORA_RANK)),
        'w_uq': nrm((N_MLA_LAYERS, Q_LORA_RANK, N_HEADS * (QK_NOPE_DIM + QK_ROPE_DIM)), Q_LORA_RANK ** -0.5),
        'w_dkv': nrm((N_MLA_LAYERS, d, KV_LORA_RANK + QK_ROPE_DIM), d ** -0.5),
        'g_kv': gain((N_MLA_LAYERS, KV_LORA_RANK)),
        'w_ukv': nrm((N_MLA_LAYERS, KV_LORA_RANK, N_HEADS * (QK_NOPE_DIM + V_HEAD_DIM)), KV_LORA_RANK ** -0.5),
        'w_o_mla': nrm((N_MLA_LAYERS, N_HEADS * V_HEAD_DIM, d), (N_HEADS * V_HEAD_DIM) ** -0.5),
        'w_conv_in': nrm((N_CONV_LAYERS, d, 3 * d), d ** -0.5),
        'w_conv': nrm((N_CONV_LAYERS, CONV_WIDTH, d), CONV_WIDTH ** -0.5),
        'w_conv_out': nrm((N_CONV_LAYERS, d, d), d ** -0.5),
        'w_router': nrm((DEPTH, d, N_EXPERTS), d ** -0.5),
        'b_router': nrm((DEPTH, N_EXPERTS), 0.01),
        'w_gu': nrm((DEPTH, N_EXPERTS, d, 2 * D_EXPERT), d ** -0.5),
        'b_gu': nrm((DEPTH, N_EXPERTS, 2 * D_EXPERT), 0.02),
        'w_down': nrm((DEPTH, N_EXPERTS, D_EXPERT, d), D_EXPERT ** -0.5),
        'b_down': nrm((DEPTH, N_EXPERTS, d), 0.02),
        'g_final': gain((d,)),
    }


def reference(x_prompt, x_sample, c_prompt, c_sample, cache_ckv, cache_krope, state_conv, page_table,
              w_ada, b_ada, g_mix_norm, g_ffn_norm,
              w_dq, g_q, w_uq, w_dkv, g_kv, w_ukv, w_o_mla,
              w_conv_in, w_conv, w_conv_out,
              w_router, b_router, w_gu, b_gu, w_down, b_down, g_final):
    xp, xs = x_prompt, x_sample
    bp, tp, d = xp.shape
    bs, ts, _ = xs.shape
    ckv_p, kr_p, ckv_s, kr_s, conv_p, conv_s = [], [], [], [], [], []
    for layer in range(DEPTH):
        sh_mp, sc_mp, ga_mp, sh_fp, sc_fp, ga_fp = adaln(c_prompt, w_ada[layer], b_ada[layer])
        sh_ms, sc_ms, ga_ms, sh_fs, sc_fs, ga_fs = adaln(c_sample, w_ada[layer], b_ada[layer])
        hp = modulate(rmsnorm(xp, g_mix_norm[layer]), sh_mp, sc_mp)
        hs = modulate(rmsnorm(xs, g_mix_norm[layer]), sh_ms, sc_ms)
        j = layer // N_MIXERS
        if layer % N_MIXERS == 0:
            mla_w = (w_dq[j], g_q[j], w_uq[j], w_dkv[j], g_kv[j], w_ukv[j], w_o_mla[j])
            yp, ck, kr = mla_prompt(hp, *mla_w)
            ys, cks, krs = mla_sample(hs, cache_ckv[j], cache_krope[j], page_table, *mla_w)
            ckv_p.append(ck)
            kr_p.append(kr)
            ckv_s.append(cks)
            kr_s.append(krs)
        else:
            zero_state = jnp.zeros((bp, CONV_WIDTH - 1, d), xp.dtype)
            yp, stp = short_conv(hp, zero_state, w_conv_in[j], w_conv[j], w_conv_out[j])
            ys, sts = short_conv(hs, state_conv[j], w_conv_in[j], w_conv[j], w_conv_out[j])
            conv_p.append(stp)
            conv_s.append(sts)
        xp = xp + ga_mp * yp
        xs = xs + ga_ms * ys
        hp = modulate(rmsnorm(xp, g_ffn_norm[layer]), sh_fp, sc_fp)
        hs = modulate(rmsnorm(xs, g_ffn_norm[layer]), sh_fs, sc_fs)
        tokens = jnp.concatenate([hp.reshape(bp * tp, d), hs.reshape(bs * ts, d)], axis=0)
        f = moe(tokens, w_router[layer], b_router[layer], w_gu[layer], b_gu[layer], w_down[layer], b_down[layer])
        xp = xp + ga_fp * f[:bp * tp].reshape(bp, tp, d)
        xs = xs + ga_fs * f[bp * tp:].reshape(bs, ts, d)
    y_prompt = rmsnorm(xp, g_final)
    y_sample = rmsnorm(xs, g_final)
    return (y_prompt, y_sample, jnp.stack(ckv_p), jnp.stack(kr_p), jnp.stack(ckv_s), jnp.stack(kr_s), jnp.stack(conv_p), jnp.stack(conv_s))
```

```python
import functools
import math

import jax
import jax.numpy as jnp
from jax import lax
from jax.experimental import pallas as pl
from jax.experimental.pallas import tpu as pltpu

BF16 = jnp.bfloat16
F32 = jnp.float32

N_HEADS = 16
QK_NOPE_DIM = 128
QK_ROPE_DIM = 64
V_HEAD_DIM = 128
ROPE_THETA = 10000.0
ATTN_SCALE = (QK_NOPE_DIM + QK_ROPE_DIM) ** -0.5
TOP_K = 4
N_ADA = 6
SWIGLU_LIMIT = 7.0
SWIGLU_ALPHA = 1.702
NORM_EPS = 1e-6
NEG_INF = -1e30

LANES = 128
BF16_SUBLANES = 16
MIB = 1024 * 1024


def _params(semantics, vmem_mib=48):
    return pltpu.CompilerParams(dimension_semantics=semantics, vmem_limit_bytes=vmem_mib * MIB)


def _pick_tile(m, target):
    if m <= target:
        return m
    for t in range(target, 15, -16):
        if m % t == 0:
            return t
    return m


def _row_spec(arr, tm):
    d = arr.shape[1]
    if arr.shape[0] == 1:
        return pl.BlockSpec((1, d), lambda i: (0, 0))
    return pl.BlockSpec((tm, d), lambda i: (i, 0))


def _rms(x):
    return x * lax.rsqrt(jnp.mean(x * x, axis=-1, keepdims=True) + NORM_EPS)


def _dot(a, b):
    return jnp.dot(a, b, preferred_element_type=F32)


def _ada_kernel(c_ref, w_ref, b_ref, o_ref):
    c = c_ref[...]
    a = (c * jax.nn.sigmoid(c)).astype(BF16)
    o_ref[...] = _dot(a, w_ref[...].astype(BF16)) + b_ref[...]


def _adaln(c_all, w_ada, b_ada):
    n_layers, d, n = w_ada.shape
    mc = c_all.shape[0]
    tn = 1024
    return pl.pallas_call(
        _ada_kernel,
        grid=(n_layers, n // tn),
        in_specs=[pl.BlockSpec((mc, d), lambda l, j: (0, 0)),
                  pl.BlockSpec((None, d, tn), lambda l, j: (l, 0, j)),
                  pl.BlockSpec((None, 1, tn), lambda l, j: (l, 0, j))],
        out_specs=pl.BlockSpec((None, mc, tn), lambda l, j: (l, 0, j)),
        out_shape=jax.ShapeDtypeStruct((n_layers, mc, n), F32),
        compiler_params=_params(("parallel", "parallel")),
        name="adaln",
    )(c_all, w_ada, b_ada.reshape(n_layers, 1, n))


def _mix_norm_kernel(x_ref, g_ref, sh_ref, sc_ref, h_ref):
    y = _rms(x_ref[...]) * g_ref[...]
    h_ref[...] = (y * (1.0 + sc_ref[...]) + sh_ref[...]).astype(h_ref.dtype)


def _mix_norm(x, g, shift, scale):
    m, d = x.shape
    tm = _pick_tile(m, 256)
    return pl.pallas_call(
        _mix_norm_kernel,
        grid=(m // tm,),
        in_specs=[pl.BlockSpec((tm, d), lambda i: (i, 0)), pl.BlockSpec((1, d), lambda i: (0, 0)),
                  _row_spec(shift, tm), _row_spec(scale, tm)],
        out_specs=pl.BlockSpec((tm, d), lambda i: (i, 0)),
        out_shape=jax.ShapeDtypeStruct((m, d), BF16),
        compiler_params=_params(("parallel",)),
        name="mix_norm",
    )(x, g, shift, scale)


def _ffn_norm_kernel(x_ref, g_ref, sh_ref, sc_ref, wh_ref, wl_ref, br_ref, h_ref, lg_ref):
    y = _rms(x_ref[...]) * g_ref[...]
    h = y * (1.0 + sc_ref[...]) + sh_ref[...]
    h_hi = h.astype(BF16)
    h_lo = (h - h_hi.astype(F32)).astype(BF16)
    h_ref[...] = h_hi
    lg_ref[...] = (_dot(h_hi, wh_ref[...]) + _dot(h_hi, wl_ref[...]) + _dot(h_lo, wh_ref[...])) + br_ref[...]


def _ffn_norm(x, g, shift, scale, wr_hi, wr_lo, br):
    m, d = x.shape
    tm = _pick_tile(m, 256)
    ne = wr_hi.shape[1]
    return pl.pallas_call(
        _ffn_norm_kernel,
        grid=(m // tm,),
        in_specs=[pl.BlockSpec((tm, d), lambda i: (i, 0)), pl.BlockSpec((1, d), lambda i: (0, 0)),
                  _row_spec(shift, tm), _row_spec(scale, tm),
                  pl.BlockSpec((d, ne), lambda i: (0, 0)), pl.BlockSpec((d, ne), lambda i: (0, 0)),
                  pl.BlockSpec((1, ne), lambda i: (0, 0))],
        out_specs=[pl.BlockSpec((tm, d), lambda i: (i, 0)), pl.BlockSpec((tm, ne), lambda i: (i, 0))],
        out_shape=[jax.ShapeDtypeStruct((m, d), BF16), jax.ShapeDtypeStruct((m, ne), F32)],
        compiler_params=_params(("parallel",)),
        name="ffn_norm_router",
    )(x, g, shift, scale, wr_hi, wr_lo, br)


def _res_norm_kernel(x_ref, f_ref, ga_ref, g_ref, sh_ref, sc_ref, xo_ref, h_ref):
    x = x_ref[...] + ga_ref[...] * f_ref[...]
    xo_ref[...] = x
    y = _rms(x) * g_ref[...]
    h_ref[...] = (y * (1.0 + sc_ref[...]) + sh_ref[...]).astype(h_ref.dtype)


def _res_norm(x, f, gate, g, shift, scale):
    m, d = x.shape
    tm = _pick_tile(m, 256)
    return pl.pallas_call(
        _res_norm_kernel,
        grid=(m // tm,),
        in_specs=[pl.BlockSpec((tm, d), lambda i: (i, 0)), pl.BlockSpec((tm, d), lambda i: (i, 0)),
                  _row_spec(gate, tm), pl.BlockSpec((1, d), lambda i: (0, 0)),
                  _row_spec(shift, tm), _row_spec(scale, tm)],
        out_specs=[pl.BlockSpec((tm, d), lambda i: (i, 0)), pl.BlockSpec((tm, d), lambda i: (i, 0))],
        out_shape=[jax.ShapeDtypeStruct((m, d), F32), jax.ShapeDtypeStruct((m, d), BF16)],
        compiler_params=_params(("parallel",)),
        name="res_norm",
    )(x, f, gate, g, shift, scale)


def _res_final_kernel(x_ref, f_ref, ga_ref, g_ref, y_ref):
    x = x_ref[...] + ga_ref[...] * f_ref[...]
    y_ref[...] = _rms(x) * g_ref[...]


def _res_final(x, f, gate, g):
    m, d = x.shape
    tm = _pick_tile(m, 256)
    return pl.pallas_call(
        _res_final_kernel,
        grid=(m // tm,),
        in_specs=[pl.BlockSpec((tm, d), lambda i: (i, 0)), pl.BlockSpec((tm, d), lambda i: (i, 0)),
                  _row_spec(gate, tm), pl.BlockSpec((1, d), lambda i: (0, 0))],
        out_specs=pl.BlockSpec((tm, d), lambda i: (i, 0)),
        out_shape=jax.ShapeDtypeStruct((m, d), F32),
        compiler_params=_params(("parallel",)),
        name="res_final_norm",
    )(x, f, gate, g)


def _rope_combine(t):
    return t + pltpu.roll(t, QK_ROPE_DIM, axis=1)


def _mla_down_kernel(h_ref, w_ref, gq_ref, gkv_ref, tab_ref, cq_ref, ckv_ref, ckvb_ref, kr_ref, krb_ref, *, ql, kvl):
    acc = _dot(h_ref[...], w_ref[...])
    cq_ref[...] = (_rms(acc[:, :ql]) * gq_ref[...]).astype(BF16)
    ckv = _rms(acc[:, ql:ql + kvl]) * gkv_ref[...]
    ckv_ref[...] = ckv
    ckvb_ref[...] = ckv.astype(BF16)
    r = _rope_combine(acc[:, ql + kvl:] * tab_ref[...])
    kr_ref[...] = r[:, :QK_ROPE_DIM]
    lane = lax.broadcasted_iota(jnp.int32, r.shape, 1)
    krb_ref[...] = jnp.where(lane < QK_ROPE_DIM, r, 0.0).astype(BF16)


def _mla_down(h, w_cat, g_q, g_kv, tab, ql, kvl):
    m, d = h.shape
    n = w_cat.shape[1]
    tm = _pick_tile(m, 512)
    row = lambda w: pl.BlockSpec((tm, w), lambda i: (i, 0))
    return pl.pallas_call(
        functools.partial(_mla_down_kernel, ql=ql, kvl=kvl),
        grid=(m // tm,),
        in_specs=[row(d), pl.BlockSpec((d, n), lambda i: (0, 0)),
                  pl.BlockSpec((1, ql), lambda i: (0, 0)), pl.BlockSpec((1, kvl), lambda i: (0, 0)), row(LANES)],
        out_specs=[row(ql), row(kvl), row(kvl), row(QK_ROPE_DIM), row(LANES)],
        out_shape=[jax.ShapeDtypeStruct((m, ql), BF16), jax.ShapeDtypeStruct((m, kvl), F32),
                   jax.ShapeDtypeStruct((m, kvl), BF16), jax.ShapeDtypeStruct((m, QK_ROPE_DIM), F32),
                   jax.ShapeDtypeStruct((m, LANES), BF16)],
        compiler_params=_params(("parallel",)),
        name="mla_down",
    )(h, w_cat, g_q, g_kv, tab)


def _q_up_kernel(cq_ref, w_ref, tab_ref, q_ref):
    cq = cq_ref[...]
    tab = tab_ref[...]
    for hd in range(N_HEADS):
        acc = _dot(cq, w_ref[hd])
        q_ref[hd, :, :QK_NOPE_DIM] = acc[:, :QK_NOPE_DIM].astype(BF16)
        q_ref[hd, :, QK_NOPE_DIM:] = _rope_combine(acc[:, QK_NOPE_DIM:] * tab).astype(BF16)


def _q_up(cq, w_q, tab):
    m, ql = cq.shape
    tm = _pick_tile(m, 512)
    hw = 2 * LANES
    return pl.pallas_call(
        _q_up_kernel,
        grid=(m // tm,),
        in_specs=[pl.BlockSpec((tm, ql), lambda i: (i, 0)),
                  pl.BlockSpec((N_HEADS, ql, hw), lambda i: (0, 0, 0)),
                  pl.BlockSpec((tm, LANES), lambda i: (i, 0))],
        out_specs=pl.BlockSpec((N_HEADS, tm, hw), lambda i: (0, i, 0)),
        out_shape=jax.ShapeDtypeStruct((N_HEADS, m, hw), BF16),
        compiler_params=_params(("parallel",)),
        name="q_up",
    )(cq, w_q, tab)


def _kv_up_kernel(ckv_ref, w_ref, krb_ref, k_ref, v_ref):
    ckv = ckv_ref[...]
    krb = krb_ref[...]
    for hd in range(N_HEADS):
        acc = _dot(ckv, w_ref[hd])
        k_ref[hd, :, :QK_NOPE_DIM] = acc[:, :QK_NOPE_DIM].astype(BF16)
        k_ref[hd, :, QK_NOPE_DIM:] = krb
        v_ref[hd] = acc[:, QK_NOPE_DIM:].astype(BF16)


def _kv_up(ckv_b, w_kv, krb):
    t, kvl = ckv_b.shape
    tm = _pick_tile(t, 512)
    hw = 2 * LANES
    return pl.pallas_call(
        _kv_up_kernel,
        grid=(t // tm,),
        in_specs=[pl.BlockSpec((tm, kvl), lambda i: (i, 0)),
                  pl.BlockSpec((N_HEADS, kvl, hw), lambda i: (0, 0, 0)),
                  pl.BlockSpec((tm, LANES), lambda i: (i, 0))],
        out_specs=[pl.BlockSpec((N_HEADS, tm, hw), lambda i: (0, i, 0)),
                   pl.BlockSpec((N_HEADS, tm, V_HEAD_DIM), lambda i: (0, i, 0))],
        out_shape=[jax.ShapeDtypeStruct((N_HEADS, t, hw), BF16), jax.ShapeDtypeStruct((N_HEADS, t, V_HEAD_DIM), BF16)],
        compiler_params=_params(("parallel",)),
        name="kv_up",
    )(ckv_b, w_kv, krb)


def _flash_kernel(il_ref, jl_ref, q_ref, k_ref, v_ref, o_ref, m_sc, l_sc, acc_sc):
    p = pl.program_id(1)
    i = il_ref[p]
    j = jl_ref[p]

    @pl.when(j == 0)
    def _():
        m_sc[...] = jnp.full(m_sc.shape, NEG_INF, F32)
        l_sc[...] = jnp.zeros(l_sc.shape, F32)
        acc_sc[...] = jnp.zeros(acc_sc.shape, F32)

    def update(masked):
        s = lax.dot_general(q_ref[...], k_ref[...], (((1,), (1,)), ((), ())), preferred_element_type=F32) * ATTN_SCALE
        if masked:
            row = lax.broadcasted_iota(jnp.int32, s.shape, 0)
            col = lax.broadcasted_iota(jnp.int32, s.shape, 1)
            s = jnp.where(col <= row, s, NEG_INF)
        m_prev = m_sc[...]
        m_new = jnp.maximum(m_prev, jnp.max(s, axis=-1, keepdims=True))
        corr = jnp.exp(m_prev - m_new)
        pr = jnp.exp(s - m_new)
        l_sc[...] = l_sc[...] * corr + jnp.sum(pr, axis=-1, keepdims=True)
        acc_sc[...] = acc_sc[...] * corr + _dot(pr.astype(BF16), v_ref[...])
        m_sc[...] = m_new

    @pl.when(j < i)
    def _():
        update(False)

    @pl.when(j == i)
    def _():
        update(True)
        o_ref[...] = (acc_sc[...] / l_sc[...]).astype(o_ref.dtype)


def _flash_attention(q, k, v, t):
    tq = _pick_tile(t, 1024)
    nq = t // tq
    pairs = [(i, j) for i in range(nq) for j in range(i + 1)]
    il = jnp.asarray([p[0] for p in pairs], jnp.int32)
    jl = jnp.asarray([p[1] for p in pairs], jnp.int32)
    hw = q.shape[2]
    grid_spec = pltpu.PrefetchScalarGridSpec(
        num_scalar_prefetch=2,
        grid=(N_HEADS, len(pairs)),
        in_specs=[pl.BlockSpec((None, tq, hw), lambda h, p, il, jl: (h, il[p], 0)),
                  pl.BlockSpec((None, tq, hw), lambda h, p, il, jl: (h, jl[p], 0)),
                  pl.BlockSpec((None, tq, V_HEAD_DIM), lambda h, p, il, jl: (h, jl[p], 0))],
        out_specs=pl.BlockSpec((tq, V_HEAD_DIM), lambda h, p, il, jl: (il[p], h)),
        scratch_shapes=[pltpu.VMEM((tq, 1), F32), pltpu.VMEM((tq, 1), F32), pltpu.VMEM((tq, V_HEAD_DIM), F32)],
    )
    return pl.pallas_call(
        _flash_kernel,
        grid_spec=grid_spec,
        out_shape=jax.ShapeDtypeStruct((t, N_HEADS * V_HEAD_DIM), BF16),
        compiler_params=_params(("parallel", "arbitrary")),
        name="flash_attention",
    )(il, jl, q, k, v)


def _bmm_kernel(a_ref, w_ref, o_ref):
    o_ref[...] = _dot(a_ref[...], w_ref[...]).astype(o_ref.dtype)


def _bmm(a, w, out_dtype):
    nh, m, kd = a.shape
    n = w.shape[2]
    return pl.pallas_call(
        _bmm_kernel,
        grid=(nh,),
        in_specs=[pl.BlockSpec((None, m, kd), lambda h: (h, 0, 0)), pl.BlockSpec((None, kd, n), lambda h: (h, 0, 0))],
        out_specs=pl.BlockSpec((None, m, n), lambda h: (h, 0, 0)),
        out_shape=jax.ShapeDtypeStruct((nh, m, n), out_dtype),
        compiler_params=_params(("parallel",)),
        name="head_matmul",
    )(a, w)


def _decode_kernel(pt_ref, qlat_ref, qrope_ref, ckn_ref, krn_ref, acc0_ref, *rest, pp, ps):
    ck_refs = rest[:pp]
    kr_refs = rest[pp:2 * pp]
    o_ref = rest[2 * pp]
    m_sc, l_sc, acc_sc, s_sc, ckb_sc = rest[2 * pp + 1:]
    p = pl.program_id(1)
    qlat = qlat_ref[...]
    qrope = qrope_ref[...]

    @pl.when(p == 0)
    def _():
        s0 = (_dot(ckn_ref[...], qlat) + _dot(krn_ref[...], qrope)) * ATTN_SCALE
        m_sc[...] = s0[0:1]
        l_sc[...] = jnp.ones(l_sc.shape, F32)
        acc_sc[...] = acc0_ref[...]

    for jj in range(pp):
        ckb = ck_refs[jj][...].astype(BF16)
        ckb_sc[jj * ps:(jj + 1) * ps, :] = ckb
        s_sc[jj * ps:(jj + 1) * ps, :] = (_dot(ckb, qlat) + _dot(kr_refs[jj][...].astype(BF16), qrope)) * ATTN_SCALE

    s = s_sc[...]
    m_prev = m_sc[...]
    m_new = jnp.maximum(m_prev, jnp.max(s, axis=0, keepdims=True))
    corr = jnp.exp(m_prev - m_new)
    pr = jnp.exp(s - m_new)
    l_sc[...] = l_sc[...] * corr + jnp.sum(pr, axis=0, keepdims=True)
    pv = lax.dot_general(ckb_sc[...], pr.astype(BF16), (((0,), (0,)), ((), ())), preferred_element_type=F32)
    acc_sc[...] = acc_sc[...] * corr + pv
    m_sc[...] = m_new

    @pl.when(p == pl.num_programs(1) - 1)
    def _():
        o_ref[...] = acc_sc[...] / l_sc[...]


def _decode_attention(page_table, q_lat_t, q_rope_t, ckn, krn, acc0_t, cache_ckv, cache_krope, layer):
    bsz, n_pages = page_table.shape
    ps, kvl = cache_ckv.shape[2], cache_ckv.shape[3]
    rd = cache_krope.shape[3]
    pp = math.gcd(n_pages, 16)
    pt_flat = page_table.reshape(-1)
    q_spec = lambda w: pl.BlockSpec((None, w, N_HEADS), lambda b, p, pt: (b, 0, 0))
    new_spec = lambda w: pl.BlockSpec((None, BF16_SUBLANES, w), lambda b, p, pt: (b, 0, 0))

    def page_spec(width, jj):
        return pl.BlockSpec((None, None, ps, width), lambda b, p, pt: (layer, pt[b * n_pages + p * pp + jj], 0, 0))

    grid_spec = pltpu.PrefetchScalarGridSpec(
        num_scalar_prefetch=1,
        grid=(bsz, n_pages // pp),
        in_specs=[q_spec(kvl), q_spec(rd), new_spec(kvl), new_spec(rd), q_spec(kvl)]
        + [page_spec(kvl, jj) for jj in range(pp)] + [page_spec(rd, jj) for jj in range(pp)],
        out_specs=q_spec(kvl),
        scratch_shapes=[pltpu.VMEM((1, N_HEADS), F32), pltpu.VMEM((1, N_HEADS), F32), pltpu.VMEM((kvl, N_HEADS), F32),
                        pltpu.VMEM((pp * ps, N_HEADS), F32), pltpu.VMEM((pp * ps, kvl), BF16)],
    )
    return pl.pallas_call(
        functools.partial(_decode_kernel, pp=pp, ps=ps),
        grid_spec=grid_spec,
        out_shape=jax.ShapeDtypeStruct((bsz, kvl, N_HEADS), F32),
        compiler_params=_params(("parallel", "arbitrary")),
        name="decode_attention",
    )(pt_flat, q_lat_t, q_rope_t, ckn, krn, acc0_t, *([cache_ckv] * pp), *([cache_krope] * pp))


def _proj_res_kernel(a_ref, w_ref, x_ref, ga_ref, o_ref):
    o_ref[...] = x_ref[...] + ga_ref[...] * _dot(a_ref[...], w_ref[...])


def _proj_res(a, w, x, gate):
    m, kd = a.shape
    d = w.shape[1]
    tm = _pick_tile(m, 512)
    return pl.pallas_call(
        _proj_res_kernel,
        grid=(m // tm,),
        in_specs=[pl.BlockSpec((tm, kd), lambda i: (i, 0)), pl.BlockSpec((kd, d), lambda i: (0, 0)),
                  pl.BlockSpec((tm, d), lambda i: (i, 0)), _row_spec(gate, tm)],
        out_specs=pl.BlockSpec((tm, d), lambda i: (i, 0)),
        out_shape=jax.ShapeDtypeStruct((m, d), F32),
        compiler_params=_params(("parallel",)),
        name="proj_residual",
    )(a, w, x, gate)


CONV_HALO = 8


def _conv_prompt_kernel(h_ref, wb_ref, wc_ref, wx_ref, wcv_ref, st_ref, z_ref, ns_ref, ubuf, *, tm):
    i = pl.program_id(1)

    @pl.when(i == 0)
    def _():
        ubuf[0:CONV_HALO, :] = st_ref[...]

    @pl.when(i > 0)
    def _():
        ubuf[0:CONV_HALO, :] = ubuf[tm:tm + CONV_HALO, :]

    h = h_ref[...]
    b_gate = _dot(h, wb_ref[...])
    ubuf[CONV_HALO:, :] = _dot(h, wc_ref[...]) * _dot(h, wx_ref[...])
    wcv = wcv_ref[...]
    conv = (wcv[0:1] * ubuf[pl.ds(CONV_HALO - 2, tm), :] + wcv[1:2] * ubuf[pl.ds(CONV_HALO - 1, tm), :]
            + wcv[2:3] * ubuf[pl.ds(CONV_HALO, tm), :])
    z_ref[...] = (b_gate * conv).astype(z_ref.dtype)

    @pl.when(i == pl.num_programs(1) - 1)
    def _():
        ns_ref[...] = ubuf[tm:tm + CONV_HALO, :]


def _conv_prompt(h, w_in, w_conv, state8):
    t, d = h.shape
    tm = _pick_tile(t, 512)
    tn = 512
    nt = d // tn
    w_spec = lambda part: pl.BlockSpec((d, tn), lambda j, i: (0, j + part * nt))
    return pl.pallas_call(
        functools.partial(_conv_prompt_kernel, tm=tm),
        grid=(nt, t // tm),
        in_specs=[pl.BlockSpec((tm, d), lambda j, i: (i, 0)), w_spec(0), w_spec(1), w_spec(2),
                  pl.BlockSpec((w_conv.shape[0], tn), lambda j, i: (0, j)),
                  pl.BlockSpec((CONV_HALO, tn), lambda j, i: (0, j))],
        out_specs=[pl.BlockSpec((tm, tn), lambda j, i: (i, j)), pl.BlockSpec((CONV_HALO, tn), lambda j, i: (0, j))],
        out_shape=[jax.ShapeDtypeStruct((t, d), BF16), jax.ShapeDtypeStruct((CONV_HALO, d), F32)],
        scratch_shapes=[pltpu.VMEM((tm + CONV_HALO, tn), F32)],
        compiler_params=_params(("parallel", "arbitrary")),
        name="conv_prompt",
    )(h, w_in, w_in, w_in, w_conv, state8)


def _conv_sample_kernel(h_ref, wb_ref, wc_ref, wx_ref, wcv_ref, s0_ref, s1_ref, z_ref, u_ref):
    h = h_ref[...]
    b_gate = _dot(h, wb_ref[...])
    u = _dot(h, wc_ref[...]) * _dot(h, wx_ref[...])
    wcv = wcv_ref[...]
    conv = wcv[0:1] * s0_ref[...] + wcv[1:2] * s1_ref[...] + wcv[2:3] * u
    z_ref[...] = (b_gate * conv).astype(z_ref.dtype)
    u_ref[...] = u


def _conv_sample(h, w_in, w_conv, s0, s1):
    b, d = h.shape
    tn = 512
    nt = d // tn
    w_spec = lambda part: pl.BlockSpec((d, tn), lambda j: (0, j + part * nt))
    col = pl.BlockSpec((b, tn), lambda j: (0, j))
    return pl.pallas_call(
        _conv_sample_kernel,
        grid=(nt,),
        in_specs=[pl.BlockSpec((b, d), lambda j: (0, 0)), w_spec(0), w_spec(1), w_spec(2),
                  pl.BlockSpec((w_conv.shape[0], tn), lambda j: (0, j)), col, col],
        out_specs=[col, col],
        out_shape=[jax.ShapeDtypeStruct((b, d), BF16), jax.ShapeDtypeStruct((b, d), F32)],
        compiler_params=_params(("parallel",)),
        name="conv_sample",
    )(h, w_in, w_in, w_in, w_conv, s0, s1)


MOE_BLOCK = 512
MOE_TN = 512


def _moe_gu_kernel(be_ref, nu_ref, new_ref, xs_ref, wg_ref, wu_ref, bg_ref, bu_ref, a_ref, wg_sc, wu_sc):
    b = pl.program_id(1)

    @pl.when(b < nu_ref[0])
    def _():
        @pl.when(new_ref[b] == 1)
        def _():
            wg_sc[...] = wg_ref[...].astype(BF16)
            wu_sc[...] = wu_ref[...].astype(BF16)

        x = xs_ref[...]
        g = jnp.minimum(_dot(x, wg_sc[...]) + bg_ref[...], SWIGLU_LIMIT)
        u = jnp.clip(_dot(x, wu_sc[...]) + bu_ref[...], -SWIGLU_LIMIT, SWIGLU_LIMIT)
        a_ref[...] = ((u + 1.0) * (g * jax.nn.sigmoid(SWIGLU_ALPHA * g))).astype(a_ref.dtype)


def _moe_down_kernel(be_ref, nu_ref, new_ref, a_ref, w_ref, b_ref, gt_ref, y_ref, w_sc):
    b = pl.program_id(1)

    @pl.when(b < nu_ref[0])
    def _():
        @pl.when(new_ref[b] == 1)
        def _():
            w_sc[...] = w_ref[...].astype(BF16)

        y_ref[...] = (_dot(a_ref[...], w_sc[...]) + b_ref[...]) * gt_ref[...]


def _moe_experts(xs, row_gate, blk_expert, n_used, blk_new, w_gu, b_gu, w_down, b_down, layer):
    r, d = xs.shape
    de = w_down.shape[2]
    nb = r // MOE_BLOCK
    tn = MOE_TN
    ntg = de // tn
    blk = lambda b, nu: jnp.minimum(b, nu[0] - 1)

    act = pl.pallas_call(
        _moe_gu_kernel,
        grid_spec=pltpu.PrefetchScalarGridSpec(
            num_scalar_prefetch=3,
            grid=(ntg, nb),
            in_specs=[pl.BlockSpec((MOE_BLOCK, d), lambda j, b, be, nu, nw: (blk(b, nu), 0)),
                      pl.BlockSpec((None, None, d, tn), lambda j, b, be, nu, nw: (layer, be[blk(b, nu)], 0, j)),
                      pl.BlockSpec((None, None, d, tn), lambda j, b, be, nu, nw: (layer, be[blk(b, nu)], 0, j + ntg)),
                      pl.BlockSpec((None, None, 1, tn), lambda j, b, be, nu, nw: (layer, be[blk(b, nu)], 0, j)),
                      pl.BlockSpec((None, None, 1, tn), lambda j, b, be, nu, nw: (layer, be[blk(b, nu)], 0, j + ntg))],
            out_specs=pl.BlockSpec((MOE_BLOCK, tn), lambda j, b, be, nu, nw: (blk(b, nu), j)),
            scratch_shapes=[pltpu.VMEM((d, tn), BF16), pltpu.VMEM((d, tn), BF16)],
        ),
        out_shape=jax.ShapeDtypeStruct((r, de), BF16),
        compiler_params=_params(("arbitrary", "arbitrary")),
        name="moe_gate_up",
    )(blk_expert, n_used, blk_new, xs, w_gu, w_gu, b_gu, b_gu)

    ntd = d // tn
    return pl.pallas_call(
        _moe_down_kernel,
        grid_spec=pltpu.PrefetchScalarGridSpec(
            num_scalar_prefetch=3,
            grid=(ntd, nb),
            in_specs=[pl.BlockSpec((MOE_BLOCK, de), lambda j, b, be, nu, nw: (blk(b, nu), 0)),
                      pl.BlockSpec((None, None, de, tn), lambda j, b, be, nu, nw: (layer, be[blk(b, nu)], 0, j)),
                      pl.BlockSpec((None, None, 1, tn), lambda j, b, be, nu, nw: (layer, be[blk(b, nu)], 0, j)),
                      pl.BlockSpec((MOE_BLOCK, 1), lambda j, b, be, nu, nw: (blk(b, nu), 0))],
            out_specs=pl.BlockSpec((MOE_BLOCK, tn), lambda j, b, be, nu, nw: (blk(b, nu), j)),
            scratch_shapes=[pltpu.VMEM((de, tn), BF16)],
        ),
        out_shape=jax.ShapeDtypeStruct((r, d), F32),
        compiler_params=_params(("arbitrary", "arbitrary")),
        name="moe_down",
    )(blk_expert, n_used, blk_new, act, w_down, b_down, row_gate)


def _moe(tokens, logits, w_gu, b_gu, w_down, b_down, layer):
    n, d = tokens.shape
    ne = logits.shape[1]
    top_val, top_idx = lax.top_k(logits, TOP_K)
    gates = jax.nn.softmax(top_val, axis=-1)
    n_assign = n * TOP_K
    flat_e = top_idx.reshape(-1)
    flat_tok = jnp.repeat(jnp.arange(n, dtype=jnp.int32), TOP_K)
    order = jnp.argsort(flat_e, stable=True)
    e_sorted = flat_e[order]
    counts = jnp.zeros((ne,), jnp.int32).at[flat_e].add(1)
    padded = (counts + MOE_BLOCK - 1) // MOE_BLOCK * MOE_BLOCK
    pad_end = jnp.cumsum(padded)
    start = jnp.cumsum(counts) - counts
    dest = (pad_end - padded)[e_sorted] + (jnp.arange(n_assign, dtype=jnp.int32) - start[e_sorted])
    n_blocks = pl.cdiv(n_assign, MOE_BLOCK) + ne
    n_rows = n_blocks * MOE_BLOCK
    row_tok = jnp.full((n_rows,), n, jnp.int32).at[dest].set(flat_tok[order])
    row_gate = jnp.zeros((n_rows,), F32).at[dest].set(gates.reshape(-1)[order])
    pos = jnp.zeros((n_assign,), jnp.int32).at[order].set(dest).reshape(n, TOP_K)
    blk_expert = jnp.minimum(
        jnp.searchsorted(pad_end, jnp.arange(n_blocks, dtype=jnp.int32) * MOE_BLOCK, side='right'), ne - 1).astype(jnp.int32)
    n_used = (pad_end[-1] // MOE_BLOCK).astype(jnp.int32).reshape(1)
    blk_new = jnp.concatenate([jnp.ones((1,), jnp.int32), (blk_expert[1:] != blk_expert[:-1]).astype(jnp.int32)])

    xs = jnp.concatenate([tokens, jnp.zeros((1, d), tokens.dtype)], axis=0)[row_tok]
    n_layers, _, de, _ = w_down.shape
    y = _moe_experts(xs, row_gate.reshape(n_rows, 1), blk_expert, n_used, blk_new,
                     w_gu, b_gu.reshape(n_layers, ne, 1, 2 * de), w_down, b_down.reshape(n_layers, ne, 1, d), layer)
    return jnp.sum(y[pos], axis=1)


def _rope_table(pos):
    half = QK_ROPE_DIM // 2
    inv = jnp.exp(-math.log(ROPE_THETA) * jnp.arange(half, dtype=F32) / half)
    ang = pos.astype(F32)[:, None] * inv[None, :]
    cos, sin = jnp.cos(ang), jnp.sin(ang)
    return jnp.concatenate([cos, cos, sin, sin], axis=-1)


def _rot_cols(w):
    half = QK_ROPE_DIM // 2
    return jnp.concatenate([-w[..., half:], w[..., :half]], axis=-1)


def kernel(x_prompt, x_sample, c_prompt, c_sample, cache_ckv, cache_krope, state_conv, page_table, w_ada, b_ada, g_mix_norm, g_ffn_norm, w_dq, g_q, w_uq, w_dkv, g_kv, w_ukv, w_o_mla, w_conv_in, w_conv, w_conv_out, w_router, b_router, w_gu, b_gu, w_down, b_down, g_final):
    bp, tp, d = x_prompt.shape
    bs, ts, _ = x_sample.shape
    assert bp == 1 and ts == 1
    depth = w_ada.shape[0]
    ne = w_router.shape[2]
    ql = w_dq.shape[2]
    kvl = g_kv.shape[1]
    past_len = page_table.shape[1] * cache_ckv.shape[2]

    xp = x_prompt.reshape(tp, d)
    xs = x_sample.reshape(bs, d)

    n_c = 1 + bs
    mc = pl.cdiv(n_c, BF16_SUBLANES) * BF16_SUBLANES
    c_all = jnp.concatenate([c_prompt, c_sample, jnp.zeros((mc - n_c, d), F32)], axis=0)
    mod = _adaln(c_all, w_ada, b_ada)

    tab_p = _rope_table(jnp.arange(tp, dtype=jnp.int32))
    tab_s = _rope_table(jnp.full((bs,), past_len, jnp.int32))

    ckv_p, kr_p, ckv_s, kr_s, conv_p, conv_s = [], [], [], [], [], []
    hp = hs = None
    for layer in range(depth):
        mp = [mod[layer, 0:1, k * d:(k + 1) * d] for k in range(N_ADA)]
        ms = [mod[layer, 1:1 + bs, k * d:(k + 1) * d] for k in range(N_ADA)]
        g_mix = g_mix_norm[layer].reshape(1, d)
        g_ffn = g_ffn_norm[layer].reshape(1, d)
        if layer == 0:
            hp = _mix_norm(xp, g_mix, mp[0], mp[1])
            hs = _mix_norm(xs, g_mix, ms[0], ms[1])
        j = layer // 2
        if layer % 2 == 0:
            w_rope = w_dkv[j][:, kvl:]
            w_cat = jnp.concatenate([w_dq[j], w_dkv[j][:, :kvl], w_rope, _rot_cols(w_rope)], axis=1).astype(BF16)
            wq = w_uq[j].reshape(ql, N_HEADS, QK_NOPE_DIM + QK_ROPE_DIM)
            wq_rope = wq[..., QK_NOPE_DIM:]
            wq = jnp.concatenate([wq[..., :QK_NOPE_DIM], wq_rope, _rot_cols(wq_rope)], axis=-1)
            wq = jnp.transpose(wq, (1, 0, 2)).astype(BF16)
            wkv3 = w_ukv[j].reshape(kvl, N_HEADS, QK_NOPE_DIM + V_HEAD_DIM)
            wkv = jnp.transpose(wkv3, (1, 0, 2)).astype(BF16)
            w_uk_t = jnp.transpose(wkv3[..., :QK_NOPE_DIM], (1, 2, 0)).astype(BF16)
            w_uv = jnp.transpose(wkv3[..., QK_NOPE_DIM:], (1, 0, 2)).astype(BF16)
            w_o = w_o_mla[j].astype(BF16)
            gq = g_q[j].reshape(1, ql)
            gkv = g_kv[j].reshape(1, kvl)

            cq, ckv, ckv_b, kr, kr_b = _mla_down(hp, w_cat, gq, gkv, tab_p, ql, kvl)
            q = _q_up(cq, wq, tab_p)
            k, v = _kv_up(ckv_b, wkv, kr_b)
            o = _flash_attention(q, k, v, tp)
            xp = _proj_res(o, w_o, xp, mp[2])
            ckv_p.append(ckv.reshape(bp, tp, kvl))
            kr_p.append(kr.reshape(bp, tp, QK_ROPE_DIM))

            cq_s, ckv_sn, ckv_sb, kr_sn, kr_sb = _mla_down(hs, w_cat, gq, gkv, tab_s, ql, kvl)
            q_s = _q_up(cq_s, wq, tab_s)
            q_lat = _bmm(q_s[:, :, :QK_NOPE_DIM], w_uk_t, BF16)
            q_lat_t = jnp.transpose(q_lat, (1, 2, 0))
            q_rope_t = jnp.transpose(q_s[:, :, QK_NOPE_DIM:QK_NOPE_DIM + QK_ROPE_DIM], (1, 2, 0))
            ckn = jnp.broadcast_to(ckv_sb[:, None, :], (bs, BF16_SUBLANES, kvl))
            krn = jnp.broadcast_to(kr_sb[:, None, :QK_ROPE_DIM], (bs, BF16_SUBLANES, QK_ROPE_DIM))
            acc0_t = jnp.broadcast_to(ckv_sb.astype(F32)[:, :, None], (bs, kvl, N_HEADS))
            o_lat_t = _decode_attention(page_table, q_lat_t, q_rope_t, ckn, krn, acc0_t, cache_ckv, cache_krope, j)
            o_lat = jnp.transpose(o_lat_t, (2, 0, 1)).astype(BF16)
            o_s = _bmm(o_lat, w_uv, BF16)
            o_s = jnp.transpose(o_s, (1, 0, 2)).reshape(bs, N_HEADS * V_HEAD_DIM)
            xs = _proj_res(o_s, w_o, xs, ms[2])
            ckv_s.append(ckv_sn.reshape(bs, ts, kvl))
            kr_s.append(kr_sn.reshape(bs, ts, QK_ROPE_DIM))
        else:
            w_in = w_conv_in[j].astype(BF16)
            w_out = w_conv_out[j].astype(BF16)
            zero_state = jnp.zeros((CONV_HALO, d), F32)
            z_p, ns_p = _conv_prompt(hp, w_in, w_conv[j], zero_state)
            xp = _proj_res(z_p, w_out, xp, mp[2])
            conv_p.append(ns_p[CONV_HALO - (w_conv.shape[1] - 1):].reshape(bp, w_conv.shape[1] - 1, d))
            s0, s1 = state_conv[j][:, 0, :], state_conv[j][:, 1, :]
            z_s, u_s = _conv_sample(hs, w_in, w_conv[j], s0, s1)
            xs = _proj_res(z_s, w_out, xs, ms[2])
            conv_s.append(jnp.stack([s1, u_s], axis=1))

        wr = jnp.pad(w_router[layer], ((0, 0), (0, LANES - ne)))
        wr_hi = wr.astype(BF16)
        wr_lo = (wr - wr_hi.astype(F32)).astype(BF16)
        br = jnp.pad(b_router[layer], (0, LANES - ne)).reshape(1, LANES)
        h2p, lg_p = _ffn_norm(xp, g_ffn, mp[3], mp[4], wr_hi, wr_lo, br)
        h2s, lg_s = _ffn_norm(xs, g_ffn, ms[3], ms[4], wr_hi, wr_lo, br)
        tokens = jnp.concatenate([h2p, h2s], axis=0)
        logits = jnp.concatenate([lg_p, lg_s], axis=0)[:, :ne]
        f = _moe(tokens, logits, w_gu, b_gu, w_down, b_down, layer)
        if layer + 1 < depth:
            g_next = g_mix_norm[layer + 1].reshape(1, d)
            mpn = [mod[layer + 1, 0:1, k * d:(k + 1) * d] for k in range(2)]
            msn = [mod[layer + 1, 1:1 + bs, k * d:(k + 1) * d] for k in range(2)]
            xp, hp = _res_norm(xp, f[:tp], mp[5], g_next, mpn[0], mpn[1])
            xs, hs = _res_norm(xs, f[tp:], ms[5], g_next, msn[0], msn[1])
        else:
            g_fin = g_final.reshape(1, d)
            y_prompt = _res_final(xp, f[:tp], mp[5], g_fin).reshape(bp, tp, d)
            y_sample = _res_final(xs, f[tp:], ms[5], g_fin).reshape(bs, ts, d)

    return (y_prompt, y_sample, jnp.stack(ckv_p), jnp.stack(kr_p), jnp.stack(ckv_s), jnp.stack(kr_s),
            jnp.stack(conv_p), jnp.stack(conv_s))
```

```python
import functools
import math

import jax
import jax.numpy as jnp
from jax import lax
from jax.experimental import pallas as pl
from jax.experimental.pallas import tpu as pltpu

BF16 = jnp.bfloat16
F32 = jnp.float32

N_HEADS = 16
QK_NOPE_DIM = 128
QK_ROPE_DIM = 64
V_HEAD_DIM = 128
ROPE_THETA = 10000.0
ATTN_SCALE = (QK_NOPE_DIM + QK_ROPE_DIM) ** -0.5
SCORE_LOG2_SCALE = ATTN_SCALE * math.log2(math.e)
TOP_K = 4
N_ADA = 6
SWIGLU_LIMIT = 7.0
SWIGLU_ALPHA = 1.702
NORM_EPS = 1e-6
NEG_INF = -1e30

LANES = 128
BF16_SUBLANES = 16
MIB = 1024 * 1024


def _params(semantics, vmem_mib=48):
    return pltpu.CompilerParams(dimension_semantics=semantics, vmem_limit_bytes=vmem_mib * MIB)


def _pick_tile(m, target):
    if m <= target:
        return m
    for t in range(target, 15, -16):
        if m % t == 0:
            return t
    return m


def _row_spec(arr, tm):
    d = arr.shape[1]
    if arr.shape[0] == 1:
        return pl.BlockSpec((1, d), lambda i: (0, 0))
    return pl.BlockSpec((tm, d), lambda i: (i, 0))


def _rms(x):
    return x * lax.rsqrt(jnp.mean(x * x, axis=-1, keepdims=True) + NORM_EPS)


def _dot(a, b):
    return jnp.dot(a, b, preferred_element_type=F32)


def _ada_kernel(c_ref, w_ref, b_ref, o_ref):
    c = c_ref[...]
    a = (c * jax.nn.sigmoid(c)).astype(BF16)
    o_ref[...] = _dot(a, w_ref[...].astype(BF16)) + b_ref[...]


def _adaln(c_all, w_ada, b_ada):
    n_layers, d, n = w_ada.shape
    mc = c_all.shape[0]
    tn = 1024
    return pl.pallas_call(
        _ada_kernel,
        grid=(n_layers, n // tn),
        in_specs=[pl.BlockSpec((mc, d), lambda l, j: (0, 0)),
                  pl.BlockSpec((None, d, tn), lambda l, j: (l, 0, j)),
                  pl.BlockSpec((None, 1, tn), lambda l, j: (l, 0, j))],
        out_specs=pl.BlockSpec((None, mc, tn), lambda l, j: (l, 0, j)),
        out_shape=jax.ShapeDtypeStruct((n_layers, mc, n), F32),
        compiler_params=_params(("parallel", "parallel")),
        name="adaln",
    )(c_all, w_ada, b_ada.reshape(n_layers, 1, n))


def _mix_norm_kernel(x_ref, g_ref, sh_ref, sc_ref, h_ref):
    y = _rms(x_ref[...]) * g_ref[...]
    h_ref[...] = (y * (1.0 + sc_ref[...]) + sh_ref[...]).astype(h_ref.dtype)


def _mix_norm(x, g, shift, scale):
    m, d = x.shape
    tm = _pick_tile(m, 256)
    return pl.pallas_call(
        _mix_norm_kernel,
        grid=(m // tm,),
        in_specs=[pl.BlockSpec((tm, d), lambda i: (i, 0)), pl.BlockSpec((1, d), lambda i: (0, 0)),
                  _row_spec(shift, tm), _row_spec(scale, tm)],
        out_specs=pl.BlockSpec((tm, d), lambda i: (i, 0)),
        out_shape=jax.ShapeDtypeStruct((m, d), BF16),
        compiler_params=_params(("parallel",)),
        name="mix_norm",
    )(x, g, shift, scale)


def _ffn_norm_kernel(x_ref, g_ref, sh_ref, sc_ref, wh_ref, wl_ref, br_ref, tok_ref, lg_ref):
    y = _rms(x_ref[...]) * g_ref[...]
    h = y * (1.0 + sc_ref[...]) + sh_ref[...]
    h_hi = h.astype(BF16)
    h_lo = (h - h_hi.astype(F32)).astype(BF16)
    tok_ref[...] = h
    lg_ref[...] = (_dot(h_hi, wh_ref[...]) + _dot(h_hi, wl_ref[...]) + _dot(h_lo, wh_ref[...])) + br_ref[...]


def _ffn_norm(x, g, shift, scale, wr_hi, wr_lo, br):
    m, d = x.shape
    tm = _pick_tile(m, 256)
    ne = wr_hi.shape[1]
    return pl.pallas_call(
        _ffn_norm_kernel,
        grid=(m // tm,),
        in_specs=[pl.BlockSpec((tm, d), lambda i: (i, 0)), pl.BlockSpec((1, d), lambda i: (0, 0)),
                  _row_spec(shift, tm), _row_spec(scale, tm),
                  pl.BlockSpec((d, ne), lambda i: (0, 0)), pl.BlockSpec((d, ne), lambda i: (0, 0)),
                  pl.BlockSpec((1, ne), lambda i: (0, 0))],
        out_specs=[pl.BlockSpec((tm, d), lambda i: (i, 0)), pl.BlockSpec((tm, ne), lambda i: (i, 0))],
        out_shape=[jax.ShapeDtypeStruct((m, d), F32), jax.ShapeDtypeStruct((m, ne), F32)],
        compiler_params=_params(("parallel",)),
        name="ffn_norm_router",
    )(x, g, shift, scale, wr_hi, wr_lo, br)


def _combine(x_ref, f_refs, gt_ref, ga_ref):
    gt = gt_ref[...]
    f = gt[:, 0:1] * f_refs[0][...]
    for k in range(1, TOP_K):
        f = f + gt[:, k:k + 1] * f_refs[k][...]
    return x_ref[...] + ga_ref[...] * f


def _res_norm_kernel(x_ref, f0_ref, f1_ref, f2_ref, f3_ref, gt_ref, ga_ref, g_ref, sh_ref, sc_ref, xo_ref, h_ref):
    x = _combine(x_ref, (f0_ref, f1_ref, f2_ref, f3_ref), gt_ref, ga_ref)
    xo_ref[...] = x
    y = _rms(x) * g_ref[...]
    h_ref[...] = (y * (1.0 + sc_ref[...]) + sh_ref[...]).astype(h_ref.dtype)


def _res_norm(x, fs, route_gates, gate, g, shift, scale, row0):
    m, d = x.shape
    tm = _pick_tile(m, 256)
    assert row0 % tm == 0
    off = row0 // tm
    tok = lambda w: pl.BlockSpec((tm, w), lambda i: (i + off, 0))
    return pl.pallas_call(
        _res_norm_kernel,
        grid=(m // tm,),
        in_specs=[pl.BlockSpec((tm, d), lambda i: (i, 0))] + [tok(d)] * TOP_K + [tok(TOP_K),
                  _row_spec(gate, tm), pl.BlockSpec((1, d), lambda i: (0, 0)),
                  _row_spec(shift, tm), _row_spec(scale, tm)],
        out_specs=[pl.BlockSpec((tm, d), lambda i: (i, 0)), pl.BlockSpec((tm, d), lambda i: (i, 0))],
        out_shape=[jax.ShapeDtypeStruct((m, d), F32), jax.ShapeDtypeStruct((m, d), BF16)],
        compiler_params=_params(("parallel",)),
        name="res_norm",
    )(x, *fs, route_gates, gate, g, shift, scale)


def _res_final_kernel(x_ref, f0_ref, f1_ref, f2_ref, f3_ref, gt_ref, ga_ref, g_ref, y_ref):
    x = _combine(x_ref, (f0_ref, f1_ref, f2_ref, f3_ref), gt_ref, ga_ref)
    y_ref[...] = _rms(x) * g_ref[...]


def _res_final(x, fs, route_gates, gate, g, row0):
    m, d = x.shape
    tm = _pick_tile(m, 256)
    assert row0 % tm == 0
    off = row0 // tm
    tok = lambda w: pl.BlockSpec((tm, w), lambda i: (i + off, 0))
    return pl.pallas_call(
        _res_final_kernel,
        grid=(m // tm,),
        in_specs=[pl.BlockSpec((tm, d), lambda i: (i, 0))] + [tok(d)] * TOP_K + [tok(TOP_K),
                  _row_spec(gate, tm), pl.BlockSpec((1, d), lambda i: (0, 0))],
        out_specs=pl.BlockSpec((tm, d), lambda i: (i, 0)),
        out_shape=jax.ShapeDtypeStruct((m, d), F32),
        compiler_params=_params(("parallel",)),
        name="res_final_norm",
    )(x, *fs, route_gates, gate, g)


def _rope_combine(t):
    return t + pltpu.roll(t, QK_ROPE_DIM, axis=1)


def _mla_down_kernel(h_ref, w_ref, gq_ref, gkv_ref, tab_ref, cq_ref, ckv_ref, ckvb_ref, kr_ref, krb_ref, *, ql, kvl):
    acc = _dot(h_ref[...], w_ref[...])
    cq_ref[...] = (_rms(acc[:, :ql]) * gq_ref[...]).astype(BF16)
    ckv = _rms(acc[:, ql:ql + kvl]) * gkv_ref[...]
    ckv_ref[...] = ckv
    ckvb_ref[...] = ckv.astype(BF16)
    r = _rope_combine(acc[:, ql + kvl:] * tab_ref[...])
    kr_ref[...] = r[:, :QK_ROPE_DIM]
    lane = lax.broadcasted_iota(jnp.int32, r.shape, 1)
    krb_ref[...] = jnp.where(lane < QK_ROPE_DIM, r, 0.0).astype(BF16)


def _mla_down(h, w_cat, g_q, g_kv, tab, ql, kvl):
    m, d = h.shape
    n = w_cat.shape[1]
    tm = _pick_tile(m, 512)
    row = lambda w: pl.BlockSpec((tm, w), lambda i: (i, 0))
    return pl.pallas_call(
        functools.partial(_mla_down_kernel, ql=ql, kvl=kvl),
        grid=(m // tm,),
        in_specs=[row(d), pl.BlockSpec((d, n), lambda i: (0, 0)),
                  pl.BlockSpec((1, ql), lambda i: (0, 0)), pl.BlockSpec((1, kvl), lambda i: (0, 0)), row(LANES)],
        out_specs=[row(ql), row(kvl), row(kvl), row(QK_ROPE_DIM), row(LANES)],
        out_shape=[jax.ShapeDtypeStruct((m, ql), BF16), jax.ShapeDtypeStruct((m, kvl), F32),
                   jax.ShapeDtypeStruct((m, kvl), BF16), jax.ShapeDtypeStruct((m, QK_ROPE_DIM), F32),
                   jax.ShapeDtypeStruct((m, LANES), BF16)],
        compiler_params=_params(("parallel",)),
        name="mla_down",
    )(h, w_cat, g_q, g_kv, tab)


def _q_up_kernel(cq_ref, w_ref, tab_ref, q_ref):
    cq = cq_ref[...]
    tab = tab_ref[...]
    for hd in range(N_HEADS):
        acc = _dot(cq, w_ref[hd])
        q_ref[hd, :, :QK_NOPE_DIM] = acc[:, :QK_NOPE_DIM].astype(BF16)
        q_ref[hd, :, QK_NOPE_DIM:] = _rope_combine(acc[:, QK_NOPE_DIM:] * tab).astype(BF16)


def _q_up(cq, w_q, tab):
    m, ql = cq.shape
    tm = _pick_tile(m, 512)
    hw = 2 * LANES
    return pl.pallas_call(
        _q_up_kernel,
        grid=(m // tm,),
        in_specs=[pl.BlockSpec((tm, ql), lambda i: (i, 0)),
                  pl.BlockSpec((N_HEADS, ql, hw), lambda i: (0, 0, 0)),
                  pl.BlockSpec((tm, LANES), lambda i: (i, 0))],
        out_specs=pl.BlockSpec((N_HEADS, tm, hw), lambda i: (0, i, 0)),
        out_shape=jax.ShapeDtypeStruct((N_HEADS, m, hw), BF16),
        compiler_params=_params(("parallel",)),
        name="q_up",
    )(cq, w_q, tab)


def _kv_up_kernel(ckv_ref, w_ref, krb_ref, k_ref, v_ref):
    ckv = ckv_ref[...]
    krb = krb_ref[...]
    for hd in range(N_HEADS):
        acc = _dot(ckv, w_ref[hd])
        k_ref[hd, :, :QK_NOPE_DIM] = acc[:, :QK_NOPE_DIM].astype(BF16)
        k_ref[hd, :, QK_NOPE_DIM:] = krb
        v_ref[hd] = acc[:, QK_NOPE_DIM:].astype(BF16)


def _kv_up(ckv_b, w_kv, krb):
    t, kvl = ckv_b.shape
    tm = _pick_tile(t, 512)
    hw = 2 * LANES
    return pl.pallas_call(
        _kv_up_kernel,
        grid=(t // tm,),
        in_specs=[pl.BlockSpec((tm, kvl), lambda i: (i, 0)),
                  pl.BlockSpec((N_HEADS, kvl, hw), lambda i: (0, 0, 0)),
                  pl.BlockSpec((tm, LANES), lambda i: (i, 0))],
        out_specs=[pl.BlockSpec((N_HEADS, tm, hw), lambda i: (0, i, 0)),
                   pl.BlockSpec((N_HEADS, tm, V_HEAD_DIM), lambda i: (0, i, 0))],
        out_shape=[jax.ShapeDtypeStruct((N_HEADS, t, hw), BF16), jax.ShapeDtypeStruct((N_HEADS, t, V_HEAD_DIM), BF16)],
        compiler_params=_params(("parallel",)),
        name="kv_up",
    )(ckv_b, w_kv, krb)


def _flash_kernel(il_ref, jl_ref, q_ref, k_ref, v_ref, o_ref, m_sc, l_sc, acc_sc):
    p = pl.program_id(1)
    i = il_ref[p]
    j = jl_ref[p]

    @pl.when(j == 0)
    def _():
        m_sc[...] = jnp.full(m_sc.shape, NEG_INF, F32)
        l_sc[...] = jnp.zeros(l_sc.shape, F32)
        acc_sc[...] = jnp.zeros(acc_sc.shape, F32)

    def update(masked):
        t = lax.dot_general(k_ref[...], q_ref[...], (((1,), (1,)), ((), ())), preferred_element_type=F32) * SCORE_LOG2_SCALE
        if masked:
            key = lax.broadcasted_iota(jnp.int32, t.shape, 0)
            qry = lax.broadcasted_iota(jnp.int32, t.shape, 1)
            t = jnp.where(key <= qry, t, NEG_INF)
        m_prev = m_sc[...]
        m_new = jnp.maximum(m_prev, jnp.max(t, axis=0, keepdims=True))
        corr = jnp.exp2(m_prev - m_new)
        pr = jnp.exp2(t - m_new)
        l_sc[...] = l_sc[...] * corr + jnp.sum(pr, axis=0, keepdims=True)
        pv = lax.dot_general(v_ref[...], pr.astype(BF16), (((0,), (0,)), ((), ())), preferred_element_type=F32)
        acc_sc[...] = acc_sc[...] * corr + pv
        m_sc[...] = m_new

    @pl.when(j < i)
    def _():
        update(False)

    @pl.when(j == i)
    def _():
        update(True)
        o_ref[...] = jnp.transpose(acc_sc[...] / l_sc[...]).astype(o_ref.dtype)


def _flash_attention(q, k, v, t):
    tq = _pick_tile(t, 1024)
    nq = t // tq
    pairs = [(i, j) for i in range(nq) for j in range(i + 1)]
    il = jnp.asarray([p[0] for p in pairs], jnp.int32)
    jl = jnp.asarray([p[1] for p in pairs], jnp.int32)
    hw = q.shape[2]
    grid_spec = pltpu.PrefetchScalarGridSpec(
        num_scalar_prefetch=2,
        grid=(N_HEADS, len(pairs)),
        in_specs=[pl.BlockSpec((None, tq, hw), lambda h, p, il, jl: (h, il[p], 0)),
                  pl.BlockSpec((None, tq, hw), lambda h, p, il, jl: (h, jl[p], 0)),
                  pl.BlockSpec((None, tq, V_HEAD_DIM), lambda h, p, il, jl: (h, jl[p], 0))],
        out_specs=pl.BlockSpec((tq, V_HEAD_DIM), lambda h, p, il, jl: (il[p], h)),
        scratch_shapes=[pltpu.VMEM((1, tq), F32), pltpu.VMEM((1, tq), F32), pltpu.VMEM((V_HEAD_DIM, tq), F32)],
    )
    return pl.pallas_call(
        _flash_kernel,
        grid_spec=grid_spec,
        out_shape=jax.ShapeDtypeStruct((t, N_HEADS * V_HEAD_DIM), BF16),
        compiler_params=_params(("parallel", "arbitrary")),
        name="flash_attention",
    )(il, jl, q, k, v)


def _bmm_kernel(a_ref, w_ref, o_ref):
    o_ref[...] = _dot(a_ref[...], w_ref[...]).astype(o_ref.dtype)


def _bmm(a, w, out_dtype):
    nh, m, kd = a.shape
    n = w.shape[2]
    return pl.pallas_call(
        _bmm_kernel,
        grid=(nh,),
        in_specs=[pl.BlockSpec((None, m, kd), lambda h: (h, 0, 0)), pl.BlockSpec((None, kd, n), lambda h: (h, 0, 0))],
        out_specs=pl.BlockSpec((None, m, n), lambda h: (h, 0, 0)),
        out_shape=jax.ShapeDtypeStruct((nh, m, n), out_dtype),
        compiler_params=_params(("parallel",)),
        name="head_matmul",
    )(a, w)


def _decode_kernel(pt_ref, qlat_ref, qrope_ref, ckn_ref, krn_ref, acc0_ref, *rest, hp, ps):
    pp = 2 * hp
    ck_refs = rest[:pp]
    krt_refs = rest[pp:2 * pp]
    o_ref = rest[2 * pp]
    m_sc, l_sc, acc_sc, sa_sc, sb_sc, ca_sc, cb_sc = rest[2 * pp + 1:]
    p = pl.program_id(1)
    qlat = qlat_ref[...]
    qrope = qrope_ref[...]

    @pl.when(p == 0)
    def _():
        s0 = (_dot(ckn_ref[...], qlat) + _dot(krn_ref[...], qrope)) * SCORE_LOG2_SCALE
        m_sc[...] = s0[0:1]
        l_sc[...] = jnp.ones(l_sc.shape, F32)
        acc_sc[...] = acc0_ref[...]
        sb_sc[...] = jnp.full(sb_sc.shape, NEG_INF, F32)
        cb_sc[...] = jnp.zeros(cb_sc.shape, BF16)

    def scores(first, s_sc, c_sc):
        for jj in range(hp):
            ckb = ck_refs[first + jj][...].astype(BF16)
            c_sc[jj * ps:(jj + 1) * ps, :] = ckb
            s_rope = lax.dot_general(krt_refs[first + jj][...].astype(BF16), qrope, (((0,), (0,)), ((), ())),
                                     preferred_element_type=F32)
            s_sc[jj * ps:(jj + 1) * ps, :] = (_dot(ckb, qlat) + s_rope) * SCORE_LOG2_SCALE

    def update(s_sc, c_sc):
        t = s_sc[...]
        m_prev = m_sc[...]
        m_new = jnp.maximum(m_prev, jnp.max(t, axis=0, keepdims=True))
        corr = jnp.exp2(m_prev - m_new)
        pr = jnp.exp2(t - m_new)
        l_sc[...] = l_sc[...] * corr + jnp.sum(pr, axis=0, keepdims=True)
        pv = lax.dot_general(c_sc[...], pr.astype(BF16), (((0,), (0,)), ((), ())), preferred_element_type=F32)
        acc_sc[...] = acc_sc[...] * corr + pv
        m_sc[...] = m_new

    scores(0, sa_sc, ca_sc)
    update(sb_sc, cb_sc)
    scores(hp, sb_sc, cb_sc)
    update(sa_sc, ca_sc)

    @pl.when(p == pl.num_programs(1) - 1)
    def _():
        update(sb_sc, cb_sc)
        o_ref[...] = acc_sc[...] / l_sc[...]


def _decode_attention(page_table, q_lat_t, q_rope_t, ckn, krn, acc0_t, cache_ckv, cache_krope_t, layer):
    bsz, n_pages = page_table.shape
    ps, kvl = cache_ckv.shape[2], cache_ckv.shape[3]
    rd = cache_krope_t.shape[2]
    pp = math.gcd(n_pages, 32)
    assert pp % 2 == 0
    pt_flat = page_table.reshape(-1)
    q_spec = lambda w: pl.BlockSpec((None, w, N_HEADS), lambda b, p, pt: (b, 0, 0))
    new_spec = lambda w: pl.BlockSpec((None, BF16_SUBLANES, w), lambda b, p, pt: (b, 0, 0))

    def page_spec(rows, width, jj):
        return pl.BlockSpec((None, None, rows, width), lambda b, p, pt: (layer, pt[b * n_pages + p * pp + jj], 0, 0))

    half_rows = pp // 2 * ps
    grid_spec = pltpu.PrefetchScalarGridSpec(
        num_scalar_prefetch=1,
        grid=(bsz, n_pages // pp),
        in_specs=[q_spec(kvl), q_spec(rd), new_spec(kvl), new_spec(rd), q_spec(kvl)]
        + [page_spec(ps, kvl, jj) for jj in range(pp)] + [page_spec(rd, ps, jj) for jj in range(pp)],
        out_specs=q_spec(kvl),
        scratch_shapes=[pltpu.VMEM((1, N_HEADS), F32), pltpu.VMEM((1, N_HEADS), F32), pltpu.VMEM((kvl, N_HEADS), F32),
                        pltpu.VMEM((half_rows, N_HEADS), F32), pltpu.VMEM((half_rows, N_HEADS), F32),
                        pltpu.VMEM((half_rows, kvl), BF16), pltpu.VMEM((half_rows, kvl), BF16)],
    )
    return pl.pallas_call(
        functools.partial(_decode_kernel, hp=pp // 2, ps=ps),
        grid_spec=grid_spec,
        out_shape=jax.ShapeDtypeStruct((bsz, kvl, N_HEADS), F32),
        compiler_params=_params(("parallel", "arbitrary")),
        name="decode_attention",
    )(pt_flat, q_lat_t, q_rope_t, ckn, krn, acc0_t, *([cache_ckv] * pp), *([cache_krope_t] * pp))


def _proj_res_kernel(a_ref, w_ref, x_ref, ga_ref, o_ref):
    o_ref[...] = x_ref[...] + ga_ref[...] * _dot(a_ref[...], w_ref[...])


def _proj_res(a, w, x, gate):
    m, kd = a.shape
    d = w.shape[1]
    tm = _pick_tile(m, 512)
    return pl.pallas_call(
        _proj_res_kernel,
        grid=(m // tm,),
        in_specs=[pl.BlockSpec((tm, kd), lambda i: (i, 0)), pl.BlockSpec((kd, d), lambda i: (0, 0)),
                  pl.BlockSpec((tm, d), lambda i: (i, 0)), _row_spec(gate, tm)],
        out_specs=pl.BlockSpec((tm, d), lambda i: (i, 0)),
        out_shape=jax.ShapeDtypeStruct((m, d), F32),
        compiler_params=_params(("parallel",)),
        name="proj_residual",
    )(a, w, x, gate)


CONV_HALO = 8


def _conv_prompt_kernel(h_ref, wb_ref, wc_ref, wx_ref, wcv_ref, st_ref, z_ref, ns_ref, ubuf, *, tm):
    i = pl.program_id(1)

    @pl.when(i == 0)
    def _():
        ubuf[0:CONV_HALO, :] = st_ref[...]

    @pl.when(i > 0)
    def _():
        ubuf[0:CONV_HALO, :] = ubuf[tm:tm + CONV_HALO, :]

    h = h_ref[...]
    b_gate = _dot(h, wb_ref[...])
    ubuf[CONV_HALO:, :] = _dot(h, wc_ref[...]) * _dot(h, wx_ref[...])
    wcv = wcv_ref[...]
    conv = (wcv[0:1] * ubuf[pl.ds(CONV_HALO - 2, tm), :] + wcv[1:2] * ubuf[pl.ds(CONV_HALO - 1, tm), :]
            + wcv[2:3] * ubuf[pl.ds(CONV_HALO, tm), :])
    z_ref[...] = (b_gate * conv).astype(z_ref.dtype)

    @pl.when(i == pl.num_programs(1) - 1)
    def _():
        ns_ref[...] = ubuf[tm:tm + CONV_HALO, :]


def _conv_prompt(h, w_in, w_conv, state8):
    t, d = h.shape
    tm = _pick_tile(t, 512)
    tn = 512
    nt = d // tn
    w_spec = lambda part: pl.BlockSpec((d, tn), lambda j, i: (0, j + part * nt))
    return pl.pallas_call(
        functools.partial(_conv_prompt_kernel, tm=tm),
        grid=(nt, t // tm),
        in_specs=[pl.BlockSpec((tm, d), lambda j, i: (i, 0)), w_spec(0), w_spec(1), w_spec(2),
                  pl.BlockSpec((w_conv.shape[0], tn), lambda j, i: (0, j)),
                  pl.BlockSpec((CONV_HALO, tn), lambda j, i: (0, j))],
        out_specs=[pl.BlockSpec((tm, tn), lambda j, i: (i, j)), pl.BlockSpec((CONV_HALO, tn), lambda j, i: (0, j))],
        out_shape=[jax.ShapeDtypeStruct((t, d), BF16), jax.ShapeDtypeStruct((CONV_HALO, d), F32)],
        scratch_shapes=[pltpu.VMEM((tm + CONV_HALO, tn), F32)],
        compiler_params=_params(("parallel", "arbitrary")),
        name="conv_prompt",
    )(h, w_in, w_in, w_in, w_conv, state8)


def _conv_sample_kernel(h_ref, wb_ref, wc_ref, wx_ref, wcv_ref, s0_ref, s1_ref, z_ref, u_ref):
    h = h_ref[...]
    b_gate = _dot(h, wb_ref[...])
    u = _dot(h, wc_ref[...]) * _dot(h, wx_ref[...])
    wcv = wcv_ref[...]
    conv = wcv[0:1] * s0_ref[...] + wcv[1:2] * s1_ref[...] + wcv[2:3] * u
    z_ref[...] = (b_gate * conv).astype(z_ref.dtype)
    u_ref[...] = u


def _conv_sample(h, w_in, w_conv, s0, s1):
    b, d = h.shape
    tn = 512
    nt = d // tn
    w_spec = lambda part: pl.BlockSpec((d, tn), lambda j: (0, j + part * nt))
    col = pl.BlockSpec((b, tn), lambda j: (0, j))
    return pl.pallas_call(
        _conv_sample_kernel,
        grid=(nt,),
        in_specs=[pl.BlockSpec((b, d), lambda j: (0, 0)), w_spec(0), w_spec(1), w_spec(2),
                  pl.BlockSpec((w_conv.shape[0], tn), lambda j: (0, j)), col, col],
        out_specs=[col, col],
        out_shape=[jax.ShapeDtypeStruct((b, d), BF16), jax.ShapeDtypeStruct((b, d), F32)],
        compiler_params=_params(("parallel",)),
        name="conv_sample",
    )(h, w_in, w_in, w_in, w_conv, s0, s1)


MOE_BLOCK = 256
MOE_TN = 1024
MXU_COLS = 256


def _col_chunks(tn):
    return [slice(c, c + MXU_COLS) for c in range(0, tn, MXU_COLS)]


def _moe_gu_kernel(be_ref, nu_ref, xs_ref, wg_ref, wu_ref, bg_ref, bu_ref, a_ref):
    b = pl.program_id(1)

    @pl.when(b < nu_ref[0])
    def _():
        x = xs_ref[...].astype(BF16)
        for cols in _col_chunks(a_ref.shape[1]):
            g = jnp.minimum(_dot(x, wg_ref[:, cols].astype(BF16)) + bg_ref[:, cols], SWIGLU_LIMIT)
            u = jnp.clip(_dot(x, wu_ref[:, cols].astype(BF16)) + bu_ref[:, cols], -SWIGLU_LIMIT, SWIGLU_LIMIT)
            a_ref[:, cols] = ((u + 1.0) * (g * jax.nn.sigmoid(SWIGLU_ALPHA * g))).astype(a_ref.dtype)


def _moe_down_kernel(be_ref, nu_ref, a_ref, w_ref, b_ref, y_ref):
    b = pl.program_id(1)

    @pl.when(b < nu_ref[0])
    def _():
        a = a_ref[...]
        for cols in _col_chunks(y_ref.shape[1]):
            y_ref[:, cols] = _dot(a, w_ref[:, cols].astype(BF16)) + b_ref[:, cols]


def _moe_experts(xs, blk_expert, n_used, w_gu, b_gu, w_down, b_down, layer):
    r, d = xs.shape
    de = w_down.shape[2]
    nb = r // MOE_BLOCK
    tn = MOE_TN
    ntg = de // tn
    blk = lambda b, nu: jnp.minimum(b, nu[0] - 1)

    act = pl.pallas_call(
        _moe_gu_kernel,
        grid_spec=pltpu.PrefetchScalarGridSpec(
            num_scalar_prefetch=2,
            grid=(ntg, nb),
            in_specs=[pl.BlockSpec((MOE_BLOCK, d), lambda j, b, be, nu: (blk(b, nu), 0)),
                      pl.BlockSpec((None, None, d, tn), lambda j, b, be, nu: (layer, be[blk(b, nu)], 0, j)),
                      pl.BlockSpec((None, None, d, tn), lambda j, b, be, nu: (layer, be[blk(b, nu)], 0, j + ntg)),
                      pl.BlockSpec((None, None, 1, tn), lambda j, b, be, nu: (layer, be[blk(b, nu)], 0, j)),
                      pl.BlockSpec((None, None, 1, tn), lambda j, b, be, nu: (layer, be[blk(b, nu)], 0, j + ntg))],
            out_specs=pl.BlockSpec((MOE_BLOCK, tn), lambda j, b, be, nu: (blk(b, nu), j)),
        ),
        out_shape=jax.ShapeDtypeStruct((r, de), BF16),
        compiler_params=_params(("arbitrary", "arbitrary"), 56),
        name="moe_gate_up",
    )(blk_expert, n_used, xs, w_gu, w_gu, b_gu, b_gu)

    ntd = d // tn
    return pl.pallas_call(
        _moe_down_kernel,
        grid_spec=pltpu.PrefetchScalarGridSpec(
            num_scalar_prefetch=2,
            grid=(ntd, nb),
            in_specs=[pl.BlockSpec((MOE_BLOCK, de), lambda j, b, be, nu: (blk(b, nu), 0)),
                      pl.BlockSpec((None, None, de, tn), lambda j, b, be, nu: (layer, be[blk(b, nu)], 0, j)),
                      pl.BlockSpec((None, None, 1, tn), lambda j, b, be, nu: (layer, be[blk(b, nu)], 0, j))],
            out_specs=pl.BlockSpec((MOE_BLOCK, tn), lambda j, b, be, nu: (blk(b, nu), j)),
        ),
        out_shape=jax.ShapeDtypeStruct((r, d), F32),
        compiler_params=_params(("arbitrary", "arbitrary")),
        name="moe_down",
    )(blk_expert, n_used, act, w_down, b_down)


def _moe(tokens, logits, w_gu, b_gu, w_down, b_down, layer):
    n, d = tokens.shape
    ne = logits.shape[1]
    top_val, top_idx = lax.top_k(logits, TOP_K)
    gates = jax.nn.softmax(top_val, axis=-1)
    n_assign = n * TOP_K
    flat_e = top_idx.reshape(-1).astype(jnp.int32)
    order = jnp.argsort(flat_e, stable=True).astype(jnp.int32)
    rank = jnp.argsort(order).astype(jnp.int32)
    counts = jnp.sum((flat_e[:, None] == jnp.arange(ne, dtype=jnp.int32)[None, :]).astype(jnp.int32), axis=0)
    padded = (counts + MOE_BLOCK - 1) // MOE_BLOCK * MOE_BLOCK
    pad_end = jnp.cumsum(padded)
    pad_start = pad_end - padded
    start = jnp.cumsum(counts) - counts
    pos = (rank + (pad_start - start)[flat_e]).reshape(n, TOP_K)
    n_blocks = pl.cdiv(n_assign, MOE_BLOCK) + ne
    n_rows = n_blocks * MOE_BLOCK
    blk_expert = jnp.minimum(
        jnp.searchsorted(pad_end, jnp.arange(n_blocks, dtype=jnp.int32) * MOE_BLOCK, side='right'), ne - 1).astype(jnp.int32)
    n_used = (pad_end[-1] // MOE_BLOCK).astype(jnp.int32).reshape(1)
    row_e = jnp.repeat(blk_expert, MOE_BLOCK)
    off = jnp.arange(n_rows, dtype=jnp.int32) - pad_start[row_e]
    src = jnp.clip(start[row_e] + off, 0, n_assign - 1)
    row_tok = jnp.where(off < counts[row_e], order[src] // TOP_K, 0)

    xs = tokens[row_tok]
    n_layers, _, de, _ = w_down.shape
    y = _moe_experts(xs, blk_expert, n_used, w_gu, b_gu.reshape(n_layers, ne, 1, 2 * de), w_down,
                     b_down.reshape(n_layers, ne, 1, d), layer)
    return [y[pos[:, k]] for k in range(TOP_K)], gates


def _rope_table(pos):
    half = QK_ROPE_DIM // 2
    inv = jnp.exp(-math.log(ROPE_THETA) * jnp.arange(half, dtype=F32) / half)
    ang = pos.astype(F32)[:, None] * inv[None, :]
    cos, sin = jnp.cos(ang), jnp.sin(ang)
    return jnp.concatenate([cos, cos, sin, sin], axis=-1)


def _rot_cols(w):
    half = QK_ROPE_DIM // 2
    return jnp.concatenate([-w[..., half:], w[..., :half]], axis=-1)


def kernel(x_prompt, x_sample, c_prompt, c_sample, cache_ckv, cache_krope, state_conv, page_table, w_ada, b_ada, g_mix_norm, g_ffn_norm, w_dq, g_q, w_uq, w_dkv, g_kv, w_ukv, w_o_mla, w_conv_in, w_conv, w_conv_out, w_router, b_router, w_gu, b_gu, w_down, b_down, g_final):
    bp, tp, d = x_prompt.shape
    bs, ts, _ = x_sample.shape
    assert bp == 1 and ts == 1
    depth = w_ada.shape[0]
    ne = w_router.shape[2]
    ql = w_dq.shape[2]
    kvl = g_kv.shape[1]
    past_len = page_table.shape[1] * cache_ckv.shape[2]

    xp = x_prompt.reshape(tp, d)
    xs = x_sample.reshape(bs, d)

    n_c = 1 + bs
    mc = pl.cdiv(n_c, BF16_SUBLANES) * BF16_SUBLANES
    c_all = jnp.concatenate([c_prompt, c_sample, jnp.zeros((mc - n_c, d), F32)], axis=0)
    mod = _adaln(c_all, w_ada, b_ada)

    tab_p = _rope_table(jnp.arange(tp, dtype=jnp.int32))
    tab_s = _rope_table(jnp.full((bs,), past_len, jnp.int32))

    ckv_p, kr_p, ckv_s, kr_s, conv_p, conv_s = [], [], [], [], [], []
    hp = hs = None
    for layer in range(depth):
        mp = [mod[layer, 0:1, k * d:(k + 1) * d] for k in range(N_ADA)]
        ms = [mod[layer, 1:1 + bs, k * d:(k + 1) * d] for k in range(N_ADA)]
        g_mix = g_mix_norm[layer].reshape(1, d)
        g_ffn = g_ffn_norm[layer].reshape(1, d)
        if layer == 0:
            hp = _mix_norm(xp, g_mix, mp[0], mp[1])
            hs = _mix_norm(xs, g_mix, ms[0], ms[1])
        j = layer // 2
        if layer % 2 == 0:
            w_rope = w_dkv[j][:, kvl:]
            w_cat = jnp.concatenate([w_dq[j], w_dkv[j][:, :kvl], w_rope, _rot_cols(w_rope)], axis=1).astype(BF16)
            wq = w_uq[j].reshape(ql, N_HEADS, QK_NOPE_DIM + QK_ROPE_DIM)
            wq_rope = wq[..., QK_NOPE_DIM:]
            wq = jnp.concatenate([wq[..., :QK_NOPE_DIM], wq_rope, _rot_cols(wq_rope)], axis=-1)
            wq = jnp.transpose(wq, (1, 0, 2)).astype(BF16)
            wkv3 = w_ukv[j].reshape(kvl, N_HEADS, QK_NOPE_DIM + V_HEAD_DIM)
            wkv = jnp.transpose(wkv3, (1, 0, 2)).astype(BF16)
            w_uk_t = jnp.transpose(wkv3[..., :QK_NOPE_DIM], (1, 2, 0)).astype(BF16)
            w_uv = jnp.transpose(wkv3[..., QK_NOPE_DIM:], (1, 0, 2)).astype(BF16)
            w_o = w_o_mla[j].astype(BF16)
            gq = g_q[j].reshape(1, ql)
            gkv = g_kv[j].reshape(1, kvl)

            cq, ckv, ckv_b, kr, kr_b = _mla_down(hp, w_cat, gq, gkv, tab_p, ql, kvl)
            q = _q_up(cq, wq, tab_p)
            k, v = _kv_up(ckv_b, wkv, kr_b)
            o = _flash_attention(q, k, v, tp)
            xp = _proj_res(o, w_o, xp, mp[2])
            ckv_p.append(ckv.reshape(bp, tp, kvl))
            kr_p.append(kr.reshape(bp, tp, QK_ROPE_DIM))

            cq_s, ckv_sn, ckv_sb, kr_sn, kr_sb = _mla_down(hs, w_cat, gq, gkv, tab_s, ql, kvl)
            q_s = _q_up(cq_s, wq, tab_s)
            q_lat = _bmm(q_s[:, :, :QK_NOPE_DIM], w_uk_t, BF16)
            q_lat_t = jnp.transpose(q_lat, (1, 2, 0))
            q_rope_t = jnp.transpose(q_s[:, :, QK_NOPE_DIM:QK_NOPE_DIM + QK_ROPE_DIM], (1, 2, 0))
            ckn = jnp.broadcast_to(ckv_sb[:, None, :], (bs, BF16_SUBLANES, kvl))
            krn = jnp.broadcast_to(kr_sb[:, None, :QK_ROPE_DIM], (bs, BF16_SUBLANES, QK_ROPE_DIM))
            acc0_t = jnp.broadcast_to(ckv_sb.astype(F32)[:, :, None], (bs, kvl, N_HEADS))
            o_lat_t = _decode_attention(page_table, q_lat_t, q_rope_t, ckn, krn, acc0_t, cache_ckv,
                                        jnp.swapaxes(cache_krope, 2, 3), j)
            o_lat = jnp.transpose(o_lat_t, (2, 0, 1)).astype(BF16)
            o_s = _bmm(o_lat, w_uv, BF16)
            o_s = jnp.transpose(o_s, (1, 0, 2)).reshape(bs, N_HEADS * V_HEAD_DIM)
            xs = _proj_res(o_s, w_o, xs, ms[2])
            ckv_s.append(ckv_sn.reshape(bs, ts, kvl))
            kr_s.append(kr_sn.reshape(bs, ts, QK_ROPE_DIM))
        else:
            w_in = w_conv_in[j].astype(BF16)
            w_out = w_conv_out[j].astype(BF16)
            zero_state = jnp.zeros((CONV_HALO, d), F32)
            z_p, ns_p = _conv_prompt(hp, w_in, w_conv[j], zero_state)
            xp = _proj_res(z_p, w_out, xp, mp[2])
            conv_p.append(ns_p[CONV_HALO - (w_conv.shape[1] - 1):].reshape(bp, w_conv.shape[1] - 1, d))
            s0, s1 = state_conv[j][:, 0, :], state_conv[j][:, 1, :]
            z_s, u_s = _conv_sample(hs, w_in, w_conv[j], s0, s1)
            xs = _proj_res(z_s, w_out, xs, ms[2])
            conv_s.append(jnp.stack([s1, u_s], axis=1))

        wr = jnp.pad(w_router[layer], ((0, 0), (0, LANES - ne)))
        wr_hi = wr.astype(BF16)
        wr_lo = (wr - wr_hi.astype(F32)).astype(BF16)
        br = jnp.pad(b_router[layer], (0, LANES - ne)).reshape(1, LANES)
        h2p, lg_p = _ffn_norm(xp, g_ffn, mp[3], mp[4], wr_hi, wr_lo, br)
        h2s, lg_s = _ffn_norm(xs, g_ffn, ms[3], ms[4], wr_hi, wr_lo, br)
        tokens = jnp.concatenate([h2p, h2s], axis=0)
        logits = jnp.concatenate([lg_p, lg_s], axis=0)[:, :ne]
        fs, route_gates = _moe(tokens, logits, w_gu, b_gu, w_down, b_down, layer)
        if layer + 1 < depth:
            g_next = g_mix_norm[layer + 1].reshape(1, d)
            mpn = [mod[layer + 1, 0:1, k * d:(k + 1) * d] for k in range(2)]
            msn = [mod[layer + 1, 1:1 + bs, k * d:(k + 1) * d] for k in range(2)]
            xp, hp = _res_norm(xp, fs, route_gates, mp[5], g_next, mpn[0], mpn[1], 0)
            xs, hs = _res_norm(xs, fs, route_gates, ms[5], g_next, msn[0], msn[1], tp)
        else:
            g_fin = g_final.reshape(1, d)
            y_prompt = _res_final(xp, fs, route_gates, mp[5], g_fin, 0).reshape(bp, tp, d)
            y_sample = _res_final(xs, fs, route_gates, ms[5], g_fin, tp).reshape(bs, ts, d)

    return (y_prompt, y_sample, jnp.stack(ckv_p), jnp.stack(kr_p), jnp.stack(ckv_s), jnp.stack(kr_s),
            jnp.stack(conv_p), jnp.stack(conv_s))
```

```python
import functools
import math

import jax
import jax.numpy as jnp
from jax import lax
from jax.experimental import pallas as pl
from jax.experimental.pallas import tpu as pltpu

BF16 = jnp.bfloat16
F32 = jnp.float32

N_HEADS = 16
QK_NOPE_DIM = 128
QK_ROPE_DIM = 64
V_HEAD_DIM = 128
ROPE_THETA = 10000.0
ATTN_SCALE = (QK_NOPE_DIM + QK_ROPE_DIM) ** -0.5
SCORE_LOG2_SCALE = ATTN_SCALE * math.log2(math.e)
TOP_K = 4
N_ADA = 6
SWIGLU_LIMIT = 7.0
SWIGLU_ALPHA = 1.702
NORM_EPS = 1e-6
NEG_INF = -1e30

LANES = 128
BF16_SUBLANES = 16
MIB = 1024 * 1024


def _params(semantics, vmem_mib=48):
    return pltpu.CompilerParams(dimension_semantics=semantics, vmem_limit_bytes=vmem_mib * MIB)


def _pick_tile(m, target):
    if m <= target:
        return m
    for t in range(target, 15, -16):
        if m % t == 0:
            return t
    return m


def _row_spec(arr, tm):
    d = arr.shape[1]
    if arr.shape[0] == 1:
        return pl.BlockSpec((1, d), lambda i: (0, 0))
    return pl.BlockSpec((tm, d), lambda i: (i, 0))


def _rms(x):
    return x * lax.rsqrt(jnp.mean(x * x, axis=-1, keepdims=True) + NORM_EPS)


def _dot(a, b):
    return jnp.dot(a, b, preferred_element_type=F32)


def _ada_kernel(c_ref, w_ref, b_ref, o_ref):
    c = c_ref[...]
    a = (c * jax.nn.sigmoid(c)).astype(BF16)
    o_ref[...] = _dot(a, w_ref[...].astype(BF16)) + b_ref[...]


def _adaln(c_all, w_ada, b_ada):
    n_layers, d, n = w_ada.shape
    mc = c_all.shape[0]
    tn = 1024
    return pl.pallas_call(
        _ada_kernel,
        grid=(n_layers, n // tn),
        in_specs=[pl.BlockSpec((mc, d), lambda l, j: (0, 0)),
                  pl.BlockSpec((None, d, tn), lambda l, j: (l, 0, j)),
                  pl.BlockSpec((None, 1, tn), lambda l, j: (l, 0, j))],
        out_specs=pl.BlockSpec((None, mc, tn), lambda l, j: (l, 0, j)),
        out_shape=jax.ShapeDtypeStruct((n_layers, mc, n), F32),
        compiler_params=_params(("parallel", "parallel")),
        name="adaln",
    )(c_all, w_ada, b_ada.reshape(n_layers, 1, n))


def _mix_norm_kernel(x_ref, g_ref, sh_ref, sc_ref, h_ref):
    y = _rms(x_ref[...]) * g_ref[...]
    h_ref[...] = (y * (1.0 + sc_ref[...]) + sh_ref[...]).astype(h_ref.dtype)


def _mix_norm(x, g, shift, scale):
    m, d = x.shape
    tm = _pick_tile(m, 256)
    return pl.pallas_call(
        _mix_norm_kernel,
        grid=(m // tm,),
        in_specs=[pl.BlockSpec((tm, d), lambda i: (i, 0)), pl.BlockSpec((1, d), lambda i: (0, 0)),
                  _row_spec(shift, tm), _row_spec(scale, tm)],
        out_specs=pl.BlockSpec((tm, d), lambda i: (i, 0)),
        out_shape=jax.ShapeDtypeStruct((m, d), BF16),
        compiler_params=_params(("parallel",)),
        name="mix_norm",
    )(x, g, shift, scale)


def _ffn_norm_kernel(x_ref, g_ref, sh_ref, sc_ref, wh_ref, wl_ref, br_ref, tok_ref, lg_ref):
    y = _rms(x_ref[...]) * g_ref[...]
    h = y * (1.0 + sc_ref[...]) + sh_ref[...]
    h_hi = h.astype(BF16)
    h_lo = (h - h_hi.astype(F32)).astype(BF16)
    tok_ref[...] = h
    lg_ref[...] = (_dot(h_hi, wh_ref[...]) + _dot(h_hi, wl_ref[...]) + _dot(h_lo, wh_ref[...])) + br_ref[...]


def _ffn_norm(x, g, shift, scale, wr_hi, wr_lo, br):
    m, d = x.shape
    tm = _pick_tile(m, 256)
    ne = wr_hi.shape[1]
    return pl.pallas_call(
        _ffn_norm_kernel,
        grid=(m // tm,),
        in_specs=[pl.BlockSpec((tm, d), lambda i: (i, 0)), pl.BlockSpec((1, d), lambda i: (0, 0)),
                  _row_spec(shift, tm), _row_spec(scale, tm),
                  pl.BlockSpec((d, ne), lambda i: (0, 0)), pl.BlockSpec((d, ne), lambda i: (0, 0)),
                  pl.BlockSpec((1, ne), lambda i: (0, 0))],
        out_specs=[pl.BlockSpec((tm, d), lambda i: (i, 0)), pl.BlockSpec((tm, ne), lambda i: (i, 0))],
        out_shape=[jax.ShapeDtypeStruct((m, d), F32), jax.ShapeDtypeStruct((m, ne), F32)],
        compiler_params=_params(("parallel",)),
        name="ffn_norm_router",
    )(x, g, shift, scale, wr_hi, wr_lo, br)


def _combine(x_ref, f_refs, gt_ref, ga_ref):
    gt = gt_ref[...]
    f = gt[:, 0:1] * f_refs[0][...]
    for k in range(1, TOP_K):
        f = f + gt[:, k:k + 1] * f_refs[k][...]
    return x_ref[...] + ga_ref[...] * f


def _res_norm_kernel(x_ref, f0_ref, f1_ref, f2_ref, f3_ref, gt_ref, ga_ref, g_ref, sh_ref, sc_ref, xo_ref, h_ref):
    x = _combine(x_ref, (f0_ref, f1_ref, f2_ref, f3_ref), gt_ref, ga_ref)
    xo_ref[...] = x
    y = _rms(x) * g_ref[...]
    h_ref[...] = (y * (1.0 + sc_ref[...]) + sh_ref[...]).astype(h_ref.dtype)


def _res_norm(x, fs, route_gates, gate, g, shift, scale, row0):
    m, d = x.shape
    tm = _pick_tile(m, 256)
    assert row0 % tm == 0
    off = row0 // tm
    choice = lambda k: pl.BlockSpec((None, tm, d), lambda i: (k, i + off, 0))
    return pl.pallas_call(
        _res_norm_kernel,
        grid=(m // tm,),
        in_specs=[pl.BlockSpec((tm, d), lambda i: (i, 0))] + [choice(k) for k in range(TOP_K)]
        + [pl.BlockSpec((tm, TOP_K), lambda i: (i + off, 0)),
                  _row_spec(gate, tm), pl.BlockSpec((1, d), lambda i: (0, 0)),
                  _row_spec(shift, tm), _row_spec(scale, tm)],
        out_specs=[pl.BlockSpec((tm, d), lambda i: (i, 0)), pl.BlockSpec((tm, d), lambda i: (i, 0))],
        out_shape=[jax.ShapeDtypeStruct((m, d), F32), jax.ShapeDtypeStruct((m, d), BF16)],
        compiler_params=_params(("parallel",)),
        name="res_norm",
    )(x, *([fs] * TOP_K), route_gates, gate, g, shift, scale)


def _res_final_kernel(x_ref, f0_ref, f1_ref, f2_ref, f3_ref, gt_ref, ga_ref, g_ref, y_ref):
    x = _combine(x_ref, (f0_ref, f1_ref, f2_ref, f3_ref), gt_ref, ga_ref)
    y_ref[...] = _rms(x) * g_ref[...]


def _res_final(x, fs, route_gates, gate, g, row0):
    m, d = x.shape
    tm = _pick_tile(m, 256)
    assert row0 % tm == 0
    off = row0 // tm
    choice = lambda k: pl.BlockSpec((None, tm, d), lambda i: (k, i + off, 0))
    return pl.pallas_call(
        _res_final_kernel,
        grid=(m // tm,),
        in_specs=[pl.BlockSpec((tm, d), lambda i: (i, 0))] + [choice(k) for k in range(TOP_K)]
        + [pl.BlockSpec((tm, TOP_K), lambda i: (i + off, 0)),
                  _row_spec(gate, tm), pl.BlockSpec((1, d), lambda i: (0, 0))],
        out_specs=pl.BlockSpec((tm, d), lambda i: (i, 0)),
        out_shape=jax.ShapeDtypeStruct((m, d), F32),
        compiler_params=_params(("parallel",)),
        name="res_final_norm",
    )(x, *([fs] * TOP_K), route_gates, gate, g)


def _rope_combine(t):
    return t + pltpu.roll(t, QK_ROPE_DIM, axis=1)


def _mla_down_kernel(h_ref, w_ref, gq_ref, gkv_ref, tab_ref, cq_ref, ckv_ref, ckvb_ref, kr_ref, krb_ref, *, ql, kvl):
    acc = _dot(h_ref[...], w_ref[...])
    cq_ref[...] = (_rms(acc[:, :ql]) * gq_ref[...]).astype(BF16)
    ckv = _rms(acc[:, ql:ql + kvl]) * gkv_ref[...]
    ckv_ref[...] = ckv
    ckvb_ref[...] = ckv.astype(BF16)
    r = _rope_combine(acc[:, ql + kvl:] * tab_ref[...])
    kr_ref[...] = r[:, :QK_ROPE_DIM]
    lane = lax.broadcasted_iota(jnp.int32, r.shape, 1)
    krb_ref[...] = jnp.where(lane < QK_ROPE_DIM, r, 0.0).astype(BF16)


def _mla_down(h, w_cat, g_q, g_kv, tab, ql, kvl):
    m, d = h.shape
    n = w_cat.shape[1]
    tm = _pick_tile(m, 512)
    row = lambda w: pl.BlockSpec((tm, w), lambda i: (i, 0))
    return pl.pallas_call(
        functools.partial(_mla_down_kernel, ql=ql, kvl=kvl),
        grid=(m // tm,),
        in_specs=[row(d), pl.BlockSpec((d, n), lambda i: (0, 0)),
                  pl.BlockSpec((1, ql), lambda i: (0, 0)), pl.BlockSpec((1, kvl), lambda i: (0, 0)), row(LANES)],
        out_specs=[row(ql), row(kvl), row(kvl), row(QK_ROPE_DIM), row(LANES)],
        out_shape=[jax.ShapeDtypeStruct((m, ql), BF16), jax.ShapeDtypeStruct((m, kvl), F32),
                   jax.ShapeDtypeStruct((m, kvl), BF16), jax.ShapeDtypeStruct((m, QK_ROPE_DIM), F32),
                   jax.ShapeDtypeStruct((m, LANES), BF16)],
        compiler_params=_params(("parallel",)),
        name="mla_down",
    )(h, w_cat, g_q, g_kv, tab)


def _q_up_kernel(cq_ref, w_ref, tab_ref, q_ref):
    cq = cq_ref[...]
    tab = tab_ref[...]
    for hd in range(N_HEADS):
        acc = _dot(cq, w_ref[hd])
        q_ref[hd, :, :QK_NOPE_DIM] = acc[:, :QK_NOPE_DIM].astype(BF16)
        q_ref[hd, :, QK_NOPE_DIM:] = _rope_combine(acc[:, QK_NOPE_DIM:] * tab).astype(BF16)


def _q_up(cq, w_q, tab):
    m, ql = cq.shape
    tm = _pick_tile(m, 512)
    hw = 2 * LANES
    return pl.pallas_call(
        _q_up_kernel,
        grid=(m // tm,),
        in_specs=[pl.BlockSpec((tm, ql), lambda i: (i, 0)),
                  pl.BlockSpec((N_HEADS, ql, hw), lambda i: (0, 0, 0)),
                  pl.BlockSpec((tm, LANES), lambda i: (i, 0))],
        out_specs=pl.BlockSpec((N_HEADS, tm, hw), lambda i: (0, i, 0)),
        out_shape=jax.ShapeDtypeStruct((N_HEADS, m, hw), BF16),
        compiler_params=_params(("parallel",)),
        name="q_up",
    )(cq, w_q, tab)


def _kv_up_kernel(ckv_ref, w_ref, krb_ref, k_ref, v_ref):
    ckv = ckv_ref[...]
    krb = krb_ref[...]
    for hd in range(N_HEADS):
        acc = _dot(ckv, w_ref[hd])
        k_ref[hd, :, :QK_NOPE_DIM] = acc[:, :QK_NOPE_DIM].astype(BF16)
        k_ref[hd, :, QK_NOPE_DIM:] = krb
        v_ref[hd] = acc[:, QK_NOPE_DIM:].astype(BF16)


def _kv_up(ckv_b, w_kv, krb):
    t, kvl = ckv_b.shape
    tm = _pick_tile(t, 512)
    hw = 2 * LANES
    return pl.pallas_call(
        _kv_up_kernel,
        grid=(t // tm,),
        in_specs=[pl.BlockSpec((tm, kvl), lambda i: (i, 0)),
                  pl.BlockSpec((N_HEADS, kvl, hw), lambda i: (0, 0, 0)),
                  pl.BlockSpec((tm, LANES), lambda i: (i, 0))],
        out_specs=[pl.BlockSpec((N_HEADS, tm, hw), lambda i: (0, i, 0)),
                   pl.BlockSpec((N_HEADS, tm, V_HEAD_DIM), lambda i: (0, i, 0))],
        out_shape=[jax.ShapeDtypeStruct((N_HEADS, t, hw), BF16), jax.ShapeDtypeStruct((N_HEADS, t, V_HEAD_DIM), BF16)],
        compiler_params=_params(("parallel",)),
        name="kv_up",
    )(ckv_b, w_kv, krb)


FLASH_HEADS = 4


def _flash_kernel(il_ref, jl_ref, q_ref, k_ref, v_ref, o_ref, m_sc, l_sc, acc_sc):
    p = pl.program_id(1)
    i = il_ref[p]
    j = jl_ref[p]

    @pl.when(j == 0)
    def _():
        m_sc[...] = jnp.full(m_sc.shape, NEG_INF, F32)
        l_sc[...] = jnp.zeros(l_sc.shape, F32)
        acc_sc[...] = jnp.zeros(acc_sc.shape, F32)

    def update(masked):
        for hh in range(FLASH_HEADS):
            t = lax.dot_general(k_ref[hh], q_ref[hh], (((1,), (1,)), ((), ())), preferred_element_type=F32) * SCORE_LOG2_SCALE
            if masked:
                key = lax.broadcasted_iota(jnp.int32, t.shape, 0)
                qry = lax.broadcasted_iota(jnp.int32, t.shape, 1)
                t = jnp.where(key <= qry, t, NEG_INF)
            m_prev = m_sc[hh]
            m_new = jnp.maximum(m_prev, jnp.max(t, axis=0, keepdims=True))
            corr = jnp.exp2(m_prev - m_new)
            pr = jnp.exp2(t - m_new)
            l_sc[hh] = l_sc[hh] * corr + jnp.sum(pr, axis=0, keepdims=True)
            pv = lax.dot_general(v_ref[hh], pr.astype(BF16), (((0,), (0,)), ((), ())), preferred_element_type=F32)
            acc_sc[hh] = acc_sc[hh] * corr + pv
            m_sc[hh] = m_new

    @pl.when(j < i)
    def _():
        update(False)

    @pl.when(j == i)
    def _():
        update(True)
        for hh in range(FLASH_HEADS):
            o_ref[:, hh * V_HEAD_DIM:(hh + 1) * V_HEAD_DIM] = jnp.transpose(acc_sc[hh] / l_sc[hh]).astype(o_ref.dtype)


def _flash_attention(q, k, v, t):
    tq = _pick_tile(t, 1024)
    nq = t // tq
    pairs = [(i, j) for i in range(nq) for j in range(i + 1)]
    il = jnp.asarray([p[0] for p in pairs], jnp.int32)
    jl = jnp.asarray([p[1] for p in pairs], jnp.int32)
    hw = q.shape[2]
    hg = FLASH_HEADS
    grid_spec = pltpu.PrefetchScalarGridSpec(
        num_scalar_prefetch=2,
        grid=(N_HEADS // hg, len(pairs)),
        in_specs=[pl.BlockSpec((hg, tq, hw), lambda h, p, il, jl: (h, il[p], 0)),
                  pl.BlockSpec((hg, tq, hw), lambda h, p, il, jl: (h, jl[p], 0)),
                  pl.BlockSpec((hg, tq, V_HEAD_DIM), lambda h, p, il, jl: (h, jl[p], 0))],
        out_specs=pl.BlockSpec((tq, hg * V_HEAD_DIM), lambda h, p, il, jl: (il[p], h)),
        scratch_shapes=[pltpu.VMEM((hg, 1, tq), F32), pltpu.VMEM((hg, 1, tq), F32), pltpu.VMEM((hg, V_HEAD_DIM, tq), F32)],
    )
    return pl.pallas_call(
        _flash_kernel,
        grid_spec=grid_spec,
        out_shape=jax.ShapeDtypeStruct((t, N_HEADS * V_HEAD_DIM), BF16),
        compiler_params=_params(("parallel", "arbitrary")),
        name="flash_attention",
    )(il, jl, q, k, v)


def _bmm_kernel(a_ref, w_ref, o_ref):
    o_ref[...] = _dot(a_ref[...], w_ref[...]).astype(o_ref.dtype)


def _bmm(a, w, out_dtype):
    nh, m, kd = a.shape
    n = w.shape[2]
    return pl.pallas_call(
        _bmm_kernel,
        grid=(nh,),
        in_specs=[pl.BlockSpec((None, m, kd), lambda h: (h, 0, 0)), pl.BlockSpec((None, kd, n), lambda h: (h, 0, 0))],
        out_specs=pl.BlockSpec((None, m, n), lambda h: (h, 0, 0)),
        out_shape=jax.ShapeDtypeStruct((nh, m, n), out_dtype),
        compiler_params=_params(("parallel",)),
        name="head_matmul",
    )(a, w)


def _decode_kernel(pt_ref, qcat_ref, new_ref, acc0_ref, ck_hbm, krt_hbm, o_ref, ckbuf, krbuf, sem,
                   m_sc, l_sc, acc_sc, sa_sc, sb_sc, ca_sc, cb_sc, *, hp, ps, kvl, layer):
    pp = 2 * hp
    n_steps = pl.num_programs(1)
    p = pl.program_id(1)
    step = pl.program_id(0) * n_steps + p
    last_step = pl.num_programs(0) * n_steps - 1
    slot = step % 2
    qlat = qcat_ref[:kvl, :]
    qrope = qcat_ref[kvl:, :]

    def page_copies(st, sl, jj):
        page = pt_ref[st * pp + jj]
        return (pltpu.make_async_copy(ck_hbm.at[layer, page], ckbuf.at[sl, jj], sem.at[sl, 0]),
                pltpu.make_async_copy(krt_hbm.at[layer, page], krbuf.at[sl, jj], sem.at[sl, 1]))

    @pl.when(step == 0)
    def _():
        for jj in range(pp):
            for cp in page_copies(0, 0, jj):
                cp.start()

    for jj in range(pp):
        for cp in page_copies(step, slot, jj):
            cp.wait()

    @pl.when(p == 0)
    def _():
        s0 = (_dot(new_ref[:, :kvl], qlat) + _dot(new_ref[:, kvl:], qrope)) * SCORE_LOG2_SCALE
        m_sc[...] = s0[0:1]
        l_sc[...] = jnp.ones(l_sc.shape, F32)
        acc_sc[...] = acc0_ref[...]
        sb_sc[...] = jnp.full(sb_sc.shape, NEG_INF, F32)
        cb_sc[...] = jnp.zeros(cb_sc.shape, BF16)

    nxt = jnp.minimum(step + 1, last_step)

    def scores(first, s_sc, c_sc):
        for jj in range(first, first + hp):
            for cp in page_copies(nxt, 1 - slot, jj):
                cp.start()
            rows = slice((jj - first) * ps, (jj - first + 1) * ps)
            ckb = ckbuf[slot, jj].astype(BF16)
            c_sc[rows, :] = ckb
            s_rope = lax.dot_general(krbuf[slot, jj].astype(BF16), qrope, (((0,), (0,)), ((), ())),
                                     preferred_element_type=F32)
            s_sc[rows, :] = (_dot(ckb, qlat) + s_rope) * SCORE_LOG2_SCALE

    def update(s_sc, c_sc):
        t = s_sc[...]
        m_prev = m_sc[...]
        m_new = jnp.maximum(m_prev, jnp.max(t, axis=0, keepdims=True))
        corr = jnp.exp2(m_prev - m_new)
        pr = jnp.exp2(t - m_new)
        l_sc[...] = l_sc[...] * corr + jnp.sum(pr, axis=0, keepdims=True)
        pv = lax.dot_general(c_sc[...], pr.astype(BF16), (((0,), (0,)), ((), ())), preferred_element_type=F32)
        acc_sc[...] = acc_sc[...] * corr + pv
        m_sc[...] = m_new

    scores(0, sa_sc, ca_sc)
    update(sb_sc, cb_sc)
    scores(hp, sb_sc, cb_sc)
    update(sa_sc, ca_sc)

    @pl.when(p == n_steps - 1)
    def _():
        update(sb_sc, cb_sc)
        o_ref[...] = acc_sc[...] / l_sc[...]

    @pl.when(step == last_step)
    def _():
        for jj in range(pp):
            for cp in page_copies(nxt, 1 - slot, jj):
                cp.wait()


def _decode_attention(page_table, q_cat_t, new_kv, acc0_t, cache_ckv, cache_krope_t, layer):
    bsz, n_pages = page_table.shape
    ps, kvl = cache_ckv.shape[2], cache_ckv.shape[3]
    rd = cache_krope_t.shape[2]
    pp = math.gcd(n_pages, 32)
    assert pp % 2 == 0
    pt_flat = page_table.reshape(-1)
    seq = lambda rows, width: pl.BlockSpec((None, rows, width), lambda b, p, pt: (b, 0, 0))
    hbm = pl.BlockSpec(memory_space=pl.ANY)
    half_rows = pp // 2 * ps
    grid_spec = pltpu.PrefetchScalarGridSpec(
        num_scalar_prefetch=1,
        grid=(bsz, n_pages // pp),
        in_specs=[seq(kvl + rd, N_HEADS), seq(BF16_SUBLANES, kvl + rd), seq(kvl, N_HEADS), hbm, hbm],
        out_specs=seq(kvl, N_HEADS),
        scratch_shapes=[pltpu.VMEM((2, pp, ps, kvl), F32), pltpu.VMEM((2, pp, rd, ps), F32),
                        pltpu.SemaphoreType.DMA((2, 2)),
                        pltpu.VMEM((1, N_HEADS), F32), pltpu.VMEM((1, N_HEADS), F32), pltpu.VMEM((kvl, N_HEADS), F32),
                        pltpu.VMEM((half_rows, N_HEADS), F32), pltpu.VMEM((half_rows, N_HEADS), F32),
                        pltpu.VMEM((half_rows, kvl), BF16), pltpu.VMEM((half_rows, kvl), BF16)],
    )
    return pl.pallas_call(
        functools.partial(_decode_kernel, hp=pp // 2, ps=ps, kvl=kvl, layer=layer),
        grid_spec=grid_spec,
        out_shape=jax.ShapeDtypeStruct((bsz, kvl, N_HEADS), F32),
        compiler_params=_params(("arbitrary", "arbitrary")),
        name="decode_attention",
    )(pt_flat, q_cat_t, new_kv, acc0_t, cache_ckv, cache_krope_t)


def _proj_res_kernel(a_ref, w_ref, x_ref, ga_ref, o_ref):
    o_ref[...] = x_ref[...] + ga_ref[...] * _dot(a_ref[...], w_ref[...])


def _proj_res(a, w, x, gate):
    m, kd = a.shape
    d = w.shape[1]
    tm = _pick_tile(m, 512)
    return pl.pallas_call(
        _proj_res_kernel,
        grid=(m // tm,),
        in_specs=[pl.BlockSpec((tm, kd), lambda i: (i, 0)), pl.BlockSpec((kd, d), lambda i: (0, 0)),
                  pl.BlockSpec((tm, d), lambda i: (i, 0)), _row_spec(gate, tm)],
        out_specs=pl.BlockSpec((tm, d), lambda i: (i, 0)),
        out_shape=jax.ShapeDtypeStruct((m, d), F32),
        compiler_params=_params(("parallel",)),
        name="proj_residual",
    )(a, w, x, gate)


CONV_HALO = 8


def _conv_prompt_kernel(h_ref, wb_ref, wc_ref, wx_ref, wcv_ref, st_ref, z_ref, ns_ref, ubuf, *, tm):
    i = pl.program_id(1)

    @pl.when(i == 0)
    def _():
        ubuf[0:CONV_HALO, :] = st_ref[...]

    @pl.when(i > 0)
    def _():
        ubuf[0:CONV_HALO, :] = ubuf[tm:tm + CONV_HALO, :]

    h = h_ref[...]
    b_gate = _dot(h, wb_ref[...])
    ubuf[CONV_HALO:, :] = _dot(h, wc_ref[...]) * _dot(h, wx_ref[...])
    wcv = wcv_ref[...]
    conv = (wcv[0:1] * ubuf[pl.ds(CONV_HALO - 2, tm), :] + wcv[1:2] * ubuf[pl.ds(CONV_HALO - 1, tm), :]
            + wcv[2:3] * ubuf[pl.ds(CONV_HALO, tm), :])
    z_ref[...] = (b_gate * conv).astype(z_ref.dtype)

    @pl.when(i == pl.num_programs(1) - 1)
    def _():
        ns_ref[...] = ubuf[tm:tm + CONV_HALO, :]


def _conv_prompt(h, w_in, w_conv, state8):
    t, d = h.shape
    tm = _pick_tile(t, 512)
    tn = 512
    nt = d // tn
    w_spec = lambda part: pl.BlockSpec((d, tn), lambda j, i: (0, j + part * nt))
    return pl.pallas_call(
        functools.partial(_conv_prompt_kernel, tm=tm),
        grid=(nt, t // tm),
        in_specs=[pl.BlockSpec((tm, d), lambda j, i: (i, 0)), w_spec(0), w_spec(1), w_spec(2),
                  pl.BlockSpec((w_conv.shape[0], tn), lambda j, i: (0, j)),
                  pl.BlockSpec((CONV_HALO, tn), lambda j, i: (0, j))],
        out_specs=[pl.BlockSpec((tm, tn), lambda j, i: (i, j)), pl.BlockSpec((CONV_HALO, tn), lambda j, i: (0, j))],
        out_shape=[jax.ShapeDtypeStruct((t, d), BF16), jax.ShapeDtypeStruct((CONV_HALO, d), F32)],
        scratch_shapes=[pltpu.VMEM((tm + CONV_HALO, tn), F32)],
        compiler_params=_params(("parallel", "arbitrary")),
        name="conv_prompt",
    )(h, w_in, w_in, w_in, w_conv, state8)


def _conv_sample_kernel(h_ref, wb_ref, wc_ref, wx_ref, wcv_ref, s0_ref, s1_ref, z_ref, u_ref):
    h = h_ref[...]
    b_gate = _dot(h, wb_ref[...])
    u = _dot(h, wc_ref[...]) * _dot(h, wx_ref[...])
    wcv = wcv_ref[...]
    conv = wcv[0:1] * s0_ref[...] + wcv[1:2] * s1_ref[...] + wcv[2:3] * u
    z_ref[...] = (b_gate * conv).astype(z_ref.dtype)
    u_ref[...] = u


def _conv_sample(h, w_in, w_conv, s0, s1):
    b, d = h.shape
    tn = 512
    nt = d // tn
    w_spec = lambda part: pl.BlockSpec((d, tn), lambda j: (0, j + part * nt))
    col = pl.BlockSpec((b, tn), lambda j: (0, j))
    return pl.pallas_call(
        _conv_sample_kernel,
        grid=(nt,),
        in_specs=[pl.BlockSpec((b, d), lambda j: (0, 0)), w_spec(0), w_spec(1), w_spec(2),
                  pl.BlockSpec((w_conv.shape[0], tn), lambda j: (0, j)), col, col],
        out_specs=[col, col],
        out_shape=[jax.ShapeDtypeStruct((b, d), BF16), jax.ShapeDtypeStruct((b, d), F32)],
        compiler_params=_params(("parallel",)),
        name="conv_sample",
    )(h, w_in, w_in, w_in, w_conv, s0, s1)


MOE_BLOCK = 512
MOE_TN = 1024
MXU_COLS = 256


def _col_chunks(tn):
    return [slice(c, c + MXU_COLS) for c in range(0, tn, MXU_COLS)]


def _expert_weight_stream(w_hbm, wbuf, sem, layer, col_starts, tn, be_ref, new_ref, slot_ref, nxt_ref, b):
    def copies(e, slot):
        return [pltpu.make_async_copy(w_hbm.at[layer, e, :, pl.ds(pl.multiple_of(c0, tn), tn)],
                                      wbuf.at[slot, k], sem.at[slot, k])
                for k, c0 in enumerate(col_starts)]

    slot = slot_ref[b]

    @pl.when(new_ref[b] == 1)
    def _():
        e = be_ref[b]

        @pl.when(b == 0)
        def _():
            for cp in copies(e, slot):
                cp.start()

        for cp in copies(e, slot):
            cp.wait()
        nxt = nxt_ref[b]

        @pl.when(nxt >= 0)
        def _():
            for cp in copies(nxt, 1 - slot):
                cp.start()

    return slot


def _moe_gu_kernel(be_ref, nu_ref, new_ref, slot_ref, nxt_ref, xs_ref, w_hbm, bg_ref, bu_ref, a_ref, wbuf, sem, *, layer):
    j = pl.program_id(0)
    b = pl.program_id(1)
    tn = a_ref.shape[1]
    n_tiles = pl.num_programs(0)

    @pl.when(b < nu_ref[0])
    def _():
        slot = _expert_weight_stream(w_hbm, wbuf, sem, layer, (j * tn, (j + n_tiles) * tn), tn,
                                     be_ref, new_ref, slot_ref, nxt_ref, b)
        x = xs_ref[...].astype(BF16)
        for cols in _col_chunks(tn):
            g = jnp.minimum(_dot(x, wbuf[slot, 0, :, cols].astype(BF16)) + bg_ref[:, cols], SWIGLU_LIMIT)
            u = jnp.clip(_dot(x, wbuf[slot, 1, :, cols].astype(BF16)) + bu_ref[:, cols], -SWIGLU_LIMIT, SWIGLU_LIMIT)
            a_ref[:, cols] = ((u + 1.0) * (g * jax.nn.sigmoid(SWIGLU_ALPHA * g))).astype(a_ref.dtype)


def _moe_down_kernel(be_ref, nu_ref, new_ref, slot_ref, nxt_ref, a_ref, w_hbm, b_ref, y_ref, wbuf, sem, *, layer):
    j = pl.program_id(0)
    b = pl.program_id(1)
    tn = y_ref.shape[1]

    @pl.when(b < nu_ref[0])
    def _():
        slot = _expert_weight_stream(w_hbm, wbuf, sem, layer, (j * tn,), tn, be_ref, new_ref, slot_ref, nxt_ref, b)
        a = a_ref[...]
        for cols in _col_chunks(tn):
            y_ref[:, cols] = _dot(a, wbuf[slot, 0, :, cols].astype(BF16)) + b_ref[:, cols]


def _moe_experts(xs, blk_expert, n_used, blk_new, blk_slot, blk_next, w_gu, b_gu, w_down, b_down, layer):
    r, d = xs.shape
    de = w_down.shape[2]
    nb = r // MOE_BLOCK
    tn = MOE_TN
    ntg = de // tn
    blk = lambda b, nu: jnp.minimum(b, nu[0] - 1)
    prefetch = (blk_expert, n_used, blk_new, blk_slot, blk_next)
    hbm = pl.BlockSpec(memory_space=pl.ANY)

    act = pl.pallas_call(
        functools.partial(_moe_gu_kernel, layer=layer),
        grid_spec=pltpu.PrefetchScalarGridSpec(
            num_scalar_prefetch=len(prefetch),
            grid=(ntg, nb),
            in_specs=[pl.BlockSpec((MOE_BLOCK, d), lambda j, b, be, nu, *_: (blk(b, nu), 0)),
                      hbm,
                      pl.BlockSpec((None, None, 1, tn), lambda j, b, be, nu, *_: (layer, be[blk(b, nu)], 0, j)),
                      pl.BlockSpec((None, None, 1, tn), lambda j, b, be, nu, *_: (layer, be[blk(b, nu)], 0, j + ntg))],
            out_specs=pl.BlockSpec((MOE_BLOCK, tn), lambda j, b, be, nu, *_: (blk(b, nu), j)),
            scratch_shapes=[pltpu.VMEM((2, 2, d, tn), F32), pltpu.SemaphoreType.DMA((2, 2))],
        ),
        out_shape=jax.ShapeDtypeStruct((r, de), BF16),
        compiler_params=_params(("arbitrary", "arbitrary"), 56),
        name="moe_gate_up",
    )(*prefetch, xs, w_gu, b_gu, b_gu)

    ntd = d // tn
    return pl.pallas_call(
        functools.partial(_moe_down_kernel, layer=layer),
        grid_spec=pltpu.PrefetchScalarGridSpec(
            num_scalar_prefetch=len(prefetch),
            grid=(ntd, nb),
            in_specs=[pl.BlockSpec((MOE_BLOCK, de), lambda j, b, be, nu, *_: (blk(b, nu), 0)),
                      hbm,
                      pl.BlockSpec((None, None, 1, tn), lambda j, b, be, nu, *_: (layer, be[blk(b, nu)], 0, j))],
            out_specs=pl.BlockSpec((MOE_BLOCK, tn), lambda j, b, be, nu, *_: (blk(b, nu), j)),
            scratch_shapes=[pltpu.VMEM((2, 1, de, tn), F32), pltpu.SemaphoreType.DMA((2, 1))],
        ),
        out_shape=jax.ShapeDtypeStruct((r, d), F32),
        compiler_params=_params(("arbitrary", "arbitrary")),
        name="moe_down",
    )(*prefetch, act, w_down, b_down)


def _moe(tokens, logits, w_gu, b_gu, w_down, b_down, layer):
    n, d = tokens.shape
    ne = logits.shape[1]
    top_val, top_idx = lax.top_k(logits, TOP_K)
    gates = jax.nn.softmax(top_val, axis=-1)
    n_assign = n * TOP_K
    flat_e = top_idx.reshape(-1).astype(jnp.int32)
    order = jnp.argsort(flat_e, stable=True).astype(jnp.int32)
    rank = jnp.argsort(order).astype(jnp.int32)
    counts = jnp.sum((flat_e[:, None] == jnp.arange(ne, dtype=jnp.int32)[None, :]).astype(jnp.int32), axis=0)
    padded = (counts + MOE_BLOCK - 1) // MOE_BLOCK * MOE_BLOCK
    pad_end = jnp.cumsum(padded)
    pad_start = pad_end - padded
    start = jnp.cumsum(counts) - counts
    pos = (rank + (pad_start - start)[flat_e]).reshape(n, TOP_K)
    n_blocks = pl.cdiv(n_assign, MOE_BLOCK) + ne
    n_rows = n_blocks * MOE_BLOCK
    blk_row0 = jnp.arange(n_blocks, dtype=jnp.int32) * MOE_BLOCK
    blk_expert = jnp.minimum(jnp.sum((pad_end[None, :] <= blk_row0[:, None]).astype(jnp.int32), axis=1), ne - 1)
    n_used = (pad_end[-1] // MOE_BLOCK).astype(jnp.int32).reshape(1)
    blk_new = jnp.concatenate([jnp.ones((1,), jnp.int32), (blk_expert[1:] != blk_expert[:-1]).astype(jnp.int32)])
    blk_slot = (jnp.cumsum(blk_new) - 1) % 2
    experts = jnp.arange(ne, dtype=jnp.int32)
    first_at_or_after = lax.cummin(jnp.where(counts > 0, experts, ne), axis=0, reverse=True)
    next_expert = jnp.concatenate([first_at_or_after[1:], jnp.full((1,), ne, jnp.int32)])
    blk_next = jnp.where(next_expert < ne, next_expert, -1)[blk_expert]
    off = (blk_row0 - pad_start[blk_expert])[:, None] + jnp.arange(MOE_BLOCK, dtype=jnp.int32)[None, :]
    src = jnp.clip(start[blk_expert][:, None] + off, 0, n_assign - 1)
    row_tok = jnp.where(off < counts[blk_expert][:, None], order[src] // TOP_K, 0).reshape(n_rows)

    xs = tokens[row_tok]
    n_layers, _, de, _ = w_down.shape
    y = _moe_experts(xs, blk_expert, n_used, blk_new, blk_slot.astype(jnp.int32), blk_next.astype(jnp.int32),
                     w_gu, b_gu.reshape(n_layers, ne, 1, 2 * de), w_down, b_down.reshape(n_layers, ne, 1, d), layer)
    return y[pos.T.reshape(-1)].reshape(TOP_K, n, d), gates


def _rope_table(pos):
    half = QK_ROPE_DIM // 2
    with jax.ensure_compile_time_eval(), jax.default_device(jax.devices("cpu")[0]):
        inv = jnp.exp(-math.log(ROPE_THETA) * jnp.arange(half, dtype=F32) / half)
        ang = jnp.asarray(pos, jnp.int32).astype(F32)[:, None] * inv[None, :]
        cos, sin = jnp.cos(ang), jnp.sin(ang)
        return jnp.concatenate([cos, cos, sin, sin], axis=-1)


def _rot_cols(w):
    half = QK_ROPE_DIM // 2
    return jnp.concatenate([-w[..., half:], w[..., :half]], axis=-1)


def kernel(x_prompt, x_sample, c_prompt, c_sample, cache_ckv, cache_krope, state_conv, page_table, w_ada, b_ada, g_mix_norm, g_ffn_norm, w_dq, g_q, w_uq, w_dkv, g_kv, w_ukv, w_o_mla, w_conv_in, w_conv, w_conv_out, w_router, b_router, w_gu, b_gu, w_down, b_down, g_final):
    bp, tp, d = x_prompt.shape
    bs, ts, _ = x_sample.shape
    assert bp == 1 and ts == 1
    depth = w_ada.shape[0]
    ne = w_router.shape[2]
    ql = w_dq.shape[2]
    kvl = g_kv.shape[1]
    past_len = page_table.shape[1] * cache_ckv.shape[2]

    xp = x_prompt.reshape(tp, d)
    xs = x_sample.reshape(bs, d)

    n_c = 1 + bs
    mc = pl.cdiv(n_c, BF16_SUBLANES) * BF16_SUBLANES
    c_all = jnp.concatenate([c_prompt, c_sample, jnp.zeros((mc - n_c, d), F32)], axis=0)
    mod = _adaln(c_all, w_ada, b_ada)

    tab_p = _rope_table(list(range(tp)))
    tab_s = _rope_table([past_len] * bs)

    ckv_p, kr_p, ckv_s, kr_s, conv_p, conv_s = [], [], [], [], [], []
    hp = hs = None
    for layer in range(depth):
        mp = [mod[layer, 0:1, k * d:(k + 1) * d] for k in range(N_ADA)]
        ms = [mod[layer, 1:1 + bs, k * d:(k + 1) * d] for k in range(N_ADA)]
        g_mix = g_mix_norm[layer].reshape(1, d)
        g_ffn = g_ffn_norm[layer].reshape(1, d)
        if layer == 0:
            hp = _mix_norm(xp, g_mix, mp[0], mp[1])
            hs = _mix_norm(xs, g_mix, ms[0], ms[1])
        j = layer // 2
        if layer % 2 == 0:
            w_rope = w_dkv[j][:, kvl:]
            w_cat = jnp.concatenate([w_dq[j], w_dkv[j][:, :kvl], w_rope, _rot_cols(w_rope)], axis=1).astype(BF16)
            wq = w_uq[j].reshape(ql, N_HEADS, QK_NOPE_DIM + QK_ROPE_DIM)
            wq_rope = wq[..., QK_NOPE_DIM:]
            wq = jnp.concatenate([wq[..., :QK_NOPE_DIM], wq_rope, _rot_cols(wq_rope)], axis=-1)
            wq = jnp.transpose(wq, (1, 0, 2)).astype(BF16)
            wkv3 = w_ukv[j].reshape(kvl, N_HEADS, QK_NOPE_DIM + V_HEAD_DIM)
            wkv = jnp.transpose(wkv3, (1, 0, 2)).astype(BF16)
            w_uk_t = jnp.transpose(wkv3[..., :QK_NOPE_DIM], (1, 2, 0)).astype(BF16)
            w_uv = jnp.transpose(wkv3[..., QK_NOPE_DIM:], (1, 0, 2)).astype(BF16)
            w_o = w_o_mla[j].astype(BF16)
            gq = g_q[j].reshape(1, ql)
            gkv = g_kv[j].reshape(1, kvl)

            cq, ckv, ckv_b, kr, kr_b = _mla_down(hp, w_cat, gq, gkv, tab_p, ql, kvl)
            q = _q_up(cq, wq, tab_p)
            k, v = _kv_up(ckv_b, wkv, kr_b)
            o = _flash_attention(q, k, v, tp)
            xp = _proj_res(o, w_o, xp, mp[2])
            ckv_p.append(ckv.reshape(bp, tp, kvl))
            kr_p.append(kr.reshape(bp, tp, QK_ROPE_DIM))

            cq_s, ckv_sn, ckv_sb, kr_sn, kr_sb = _mla_down(hs, w_cat, gq, gkv, tab_s, ql, kvl)
            q_s = _q_up(cq_s, wq, tab_s)
            q_lat = _bmm(q_s[:, :, :QK_NOPE_DIM], w_uk_t, BF16)
            q_cat = jnp.concatenate([q_lat, q_s[:, :, QK_NOPE_DIM:QK_NOPE_DIM + QK_ROPE_DIM]], axis=2)
            q_cat_t = jnp.transpose(q_cat, (1, 2, 0))
            new_kv = jnp.concatenate([ckv_sb, kr_sb[:, :QK_ROPE_DIM]], axis=1)
            new_kv = jnp.broadcast_to(new_kv[:, None, :], (bs, BF16_SUBLANES, kvl + QK_ROPE_DIM))
            acc0_t = jnp.broadcast_to(ckv_sb.astype(F32)[:, :, None], (bs, kvl, N_HEADS))
            o_lat_t = _decode_attention(page_table, q_cat_t, new_kv, acc0_t, cache_ckv,
                                        jnp.swapaxes(cache_krope, 2, 3), j)
            o_lat = jnp.transpose(o_lat_t, (2, 0, 1)).astype(BF16)
            o_s = _bmm(o_lat, w_uv, BF16)
            o_s = jnp.transpose(o_s, (1, 0, 2)).reshape(bs, N_HEADS * V_HEAD_DIM)
            xs = _proj_res(o_s, w_o, xs, ms[2])
            ckv_s.append(ckv_sn.reshape(bs, ts, kvl))
            kr_s.append(kr_sn.reshape(bs, ts, QK_ROPE_DIM))
        else:
            w_in = w_conv_in[j].astype(BF16)
            w_out = w_conv_out[j].astype(BF16)
            zero_state = jnp.zeros((CONV_HALO, d), F32)
            z_p, ns_p = _conv_prompt(hp, w_in, w_conv[j], zero_state)
            xp = _proj_res(z_p, w_out, xp, mp[2])
            conv_p.append(ns_p[CONV_HALO - (w_conv.shape[1] - 1):].reshape(bp, w_conv.shape[1] - 1, d))
            s0, s1 = state_conv[j][:, 0, :], state_conv[j][:, 1, :]
            z_s, u_s = _conv_sample(hs, w_in, w_conv[j], s0, s1)
            xs = _proj_res(z_s, w_out, xs, ms[2])
            conv_s.append(jnp.stack([s1, u_s], axis=1))

        wr = jnp.pad(w_router[layer], ((0, 0), (0, LANES - ne)))
        wr_hi = wr.astype(BF16)
        wr_lo = (wr - wr_hi.astype(F32)).astype(BF16)
        br = jnp.pad(b_router[layer], (0, LANES - ne)).reshape(1, LANES)
        h2p, lg_p = _ffn_norm(xp, g_ffn, mp[3], mp[4], wr_hi, wr_lo, br)
        h2s, lg_s = _ffn_norm(xs, g_ffn, ms[3], ms[4], wr_hi, wr_lo, br)
        tokens = jnp.concatenate([h2p, h2s], axis=0)
        logits = jnp.concatenate([lg_p, lg_s], axis=0)[:, :ne]
        fs, route_gates = _moe(tokens, logits, w_gu, b_gu, w_down, b_down, layer)
        if layer + 1 < depth:
            g_next = g_mix_norm[layer + 1].reshape(1, d)
            mpn = [mod[layer + 1, 0:1, k * d:(k + 1) * d] for k in range(2)]
            msn = [mod[layer + 1, 1:1 + bs, k * d:(k + 1) * d] for k in range(2)]
            xp, hp = _res_norm(xp, fs, route_gates, mp[5], g_next, mpn[0], mpn[1], 0)
            xs, hs = _res_norm(xs, fs, route_gates, ms[5], g_next, msn[0], msn[1], tp)
        else:
            g_fin = g_final.reshape(1, d)
            y_prompt = _res_final(xp, fs, route_gates, mp[5], g_fin, 0).reshape(bp, tp, d)
            y_sample = _res_final(xs, fs, route_gates, ms[5], g_fin, tp).reshape(bs, ts, d)

    return (y_prompt, y_sample, jnp.stack(ckv_p), jnp.stack(kr_p), jnp.stack(ckv_s), jnp.stack(kr_s),
            jnp.stack(conv_p), jnp.stack(conv_s))
```

```python
import functools
import math

import jax
import jax.numpy as jnp
from jax import lax
from jax.experimental import pallas as pl
from jax.experimental.pallas import tpu as pltpu

BF16 = jnp.bfloat16
F32 = jnp.float32

N_HEADS = 16
QK_NOPE_DIM = 128
QK_ROPE_DIM = 64
V_HEAD_DIM = 128
ROPE_THETA = 10000.0
ATTN_SCALE = (QK_NOPE_DIM + QK_ROPE_DIM) ** -0.5
SCORE_LOG2_SCALE = ATTN_SCALE * math.log2(math.e)
TOP_K = 4
N_ADA = 6
SWIGLU_LIMIT = 7.0
SWIGLU_ALPHA = 1.702
NORM_EPS = 1e-6
NEG_INF = -1e30

LANES = 128
BF16_SUBLANES = 16
MIB = 1024 * 1024


def _params(semantics, vmem_mib=48):
    return pltpu.CompilerParams(dimension_semantics=semantics, vmem_limit_bytes=vmem_mib * MIB)


def _pick_tile(m, target):
    if m <= target:
        return m
    for t in range(target, 15, -16):
        if m % t == 0:
            return t
    return m


def _row_spec(arr, tm):
    d = arr.shape[1]
    if arr.shape[0] == 1:
        return pl.BlockSpec((1, d), lambda i: (0, 0))
    return pl.BlockSpec((tm, d), lambda i: (i, 0))


def _rms(x):
    return x * lax.rsqrt(jnp.mean(x * x, axis=-1, keepdims=True) + NORM_EPS)


def _dot(a, b):
    return jnp.dot(a, b, preferred_element_type=F32)


def _ada_kernel(c_ref, w_ref, b_ref, o_ref):
    c = c_ref[...]
    a = (c * jax.nn.sigmoid(c)).astype(BF16)
    o_ref[...] = _dot(a, w_ref[...].astype(BF16)) + b_ref[...]


def _adaln(c_all, w_ada, b_ada):
    n_layers, d, n = w_ada.shape
    mc = c_all.shape[0]
    tn = 1024
    return pl.pallas_call(
        _ada_kernel,
        grid=(n_layers, n // tn),
        in_specs=[pl.BlockSpec((mc, d), lambda l, j: (0, 0)),
                  pl.BlockSpec((None, d, tn), lambda l, j: (l, 0, j)),
                  pl.BlockSpec((None, 1, tn), lambda l, j: (l, 0, j))],
        out_specs=pl.BlockSpec((None, mc, tn), lambda l, j: (l, 0, j)),
        out_shape=jax.ShapeDtypeStruct((n_layers, mc, n), F32),
        compiler_params=_params(("parallel", "parallel")),
        name="adaln",
    )(c_all, w_ada, b_ada.reshape(n_layers, 1, n))


def _mix_norm_kernel(x_ref, g_ref, sh_ref, sc_ref, h_ref):
    y = _rms(x_ref[...]) * g_ref[...]
    h_ref[...] = (y * (1.0 + sc_ref[...]) + sh_ref[...]).astype(h_ref.dtype)


def _mix_norm(x, g, shift, scale):
    m, d = x.shape
    tm = _pick_tile(m, 256)
    return pl.pallas_call(
        _mix_norm_kernel,
        grid=(m // tm,),
        in_specs=[pl.BlockSpec((tm, d), lambda i: (i, 0)), pl.BlockSpec((1, d), lambda i: (0, 0)),
                  _row_spec(shift, tm), _row_spec(scale, tm)],
        out_specs=pl.BlockSpec((tm, d), lambda i: (i, 0)),
        out_shape=jax.ShapeDtypeStruct((m, d), BF16),
        compiler_params=_params(("parallel",)),
        name="mix_norm",
    )(x, g, shift, scale)


def _ffn_norm_kernel(x_ref, g_ref, sh_ref, sc_ref, wh_ref, wl_ref, br_ref, tok_ref, lg_ref):
    y = _rms(x_ref[...]) * g_ref[...]
    h = y * (1.0 + sc_ref[...]) + sh_ref[...]
    h_hi = h.astype(BF16)
    h_lo = (h - h_hi.astype(F32)).astype(BF16)
    tok_ref[...] = h
    lg_ref[...] = (_dot(h_hi, wh_ref[...]) + _dot(h_hi, wl_ref[...]) + _dot(h_lo, wh_ref[...])) + br_ref[...]


def _ffn_norm(x, g, shift, scale, wr_hi, wr_lo, br):
    m, d = x.shape
    tm = _pick_tile(m, 256)
    ne = wr_hi.shape[1]
    return pl.pallas_call(
        _ffn_norm_kernel,
        grid=(m // tm,),
        in_specs=[pl.BlockSpec((tm, d), lambda i: (i, 0)), pl.BlockSpec((1, d), lambda i: (0, 0)),
                  _row_spec(shift, tm), _row_spec(scale, tm),
                  pl.BlockSpec((d, ne), lambda i: (0, 0)), pl.BlockSpec((d, ne), lambda i: (0, 0)),
                  pl.BlockSpec((1, ne), lambda i: (0, 0))],
        out_specs=[pl.BlockSpec((tm, d), lambda i: (i, 0)), pl.BlockSpec((tm, ne), lambda i: (i, 0))],
        out_shape=[jax.ShapeDtypeStruct((m, d), F32), jax.ShapeDtypeStruct((m, ne), F32)],
        compiler_params=_params(("parallel",)),
        name="ffn_norm_router",
    )(x, g, shift, scale, wr_hi, wr_lo, br)


def _combine(x_ref, f_refs, gt_ref, ga_ref):
    gt = gt_ref[...]
    f = gt[:, 0:1] * f_refs[0][...]
    for k in range(1, TOP_K):
        f = f + gt[:, k:k + 1] * f_refs[k][...]
    return x_ref[...] + ga_ref[...] * f


def _res_norm_kernel(x_ref, f0_ref, f1_ref, f2_ref, f3_ref, gt_ref, ga_ref, g_ref, sh_ref, sc_ref, xo_ref, h_ref):
    x = _combine(x_ref, (f0_ref, f1_ref, f2_ref, f3_ref), gt_ref, ga_ref)
    xo_ref[...] = x
    y = _rms(x) * g_ref[...]
    h_ref[...] = (y * (1.0 + sc_ref[...]) + sh_ref[...]).astype(h_ref.dtype)


def _res_norm(x, fs, route_gates, gate, g, shift, scale, row0):
    m, d = x.shape
    tm = _pick_tile(m, 256)
    assert row0 % tm == 0
    off = row0 // tm
    choice = lambda k: pl.BlockSpec((None, tm, d), lambda i: (k, i + off, 0))
    return pl.pallas_call(
        _res_norm_kernel,
        grid=(m // tm,),
        in_specs=[pl.BlockSpec((tm, d), lambda i: (i, 0))] + [choice(k) for k in range(TOP_K)]
        + [pl.BlockSpec((tm, TOP_K), lambda i: (i + off, 0)),
                  _row_spec(gate, tm), pl.BlockSpec((1, d), lambda i: (0, 0)),
                  _row_spec(shift, tm), _row_spec(scale, tm)],
        out_specs=[pl.BlockSpec((tm, d), lambda i: (i, 0)), pl.BlockSpec((tm, d), lambda i: (i, 0))],
        out_shape=[jax.ShapeDtypeStruct((m, d), F32), jax.ShapeDtypeStruct((m, d), BF16)],
        compiler_params=_params(("parallel",)),
        name="res_norm",
    )(x, *([fs] * TOP_K), route_gates, gate, g, shift, scale)


def _res_final_kernel(x_ref, f0_ref, f1_ref, f2_ref, f3_ref, gt_ref, ga_ref, g_ref, y_ref):
    x = _combine(x_ref, (f0_ref, f1_ref, f2_ref, f3_ref), gt_ref, ga_ref)
    y_ref[...] = _rms(x) * g_ref[...]


def _res_final(x, fs, route_gates, gate, g, row0):
    m, d = x.shape
    tm = _pick_tile(m, 256)
    assert row0 % tm == 0
    off = row0 // tm
    choice = lambda k: pl.BlockSpec((None, tm, d), lambda i: (k, i + off, 0))
    return pl.pallas_call(
        _res_final_kernel,
        grid=(m // tm,),
        in_specs=[pl.BlockSpec((tm, d), lambda i: (i, 0))] + [choice(k) for k in range(TOP_K)]
        + [pl.BlockSpec((tm, TOP_K), lambda i: (i + off, 0)),
                  _row_spec(gate, tm), pl.BlockSpec((1, d), lambda i: (0, 0))],
        out_specs=pl.BlockSpec((tm, d), lambda i: (i, 0)),
        out_shape=jax.ShapeDtypeStruct((m, d), F32),
        compiler_params=_params(("parallel",)),
        name="res_final_norm",
    )(x, *([fs] * TOP_K), route_gates, gate, g)


def _rope_combine(t):
    return t + pltpu.roll(t, QK_ROPE_DIM, axis=1)


def _mla_down_kernel(h_ref, w_ref, gq_ref, gkv_ref, tab_ref, cq_ref, ckv_ref, ckvb_ref, kr_ref, krb_ref, *, ql, kvl):
    acc = _dot(h_ref[...], w_ref[...])
    cq_ref[...] = (_rms(acc[:, :ql]) * gq_ref[...]).astype(BF16)
    ckv = _rms(acc[:, ql:ql + kvl]) * gkv_ref[...]
    ckv_ref[...] = ckv
    ckvb_ref[...] = ckv.astype(BF16)
    r = _rope_combine(acc[:, ql + kvl:] * tab_ref[...])
    kr_ref[...] = r[:, :QK_ROPE_DIM]
    lane = lax.broadcasted_iota(jnp.int32, r.shape, 1)
    krb_ref[...] = jnp.where(lane < QK_ROPE_DIM, r, 0.0).astype(BF16)


def _mla_down(h, w_cat, g_q, g_kv, tab, ql, kvl):
    m, d = h.shape
    n = w_cat.shape[1]
    tm = _pick_tile(m, 512)
    row = lambda w: pl.BlockSpec((tm, w), lambda i: (i, 0))
    return pl.pallas_call(
        functools.partial(_mla_down_kernel, ql=ql, kvl=kvl),
        grid=(m // tm,),
        in_specs=[row(d), pl.BlockSpec((d, n), lambda i: (0, 0)),
                  pl.BlockSpec((1, ql), lambda i: (0, 0)), pl.BlockSpec((1, kvl), lambda i: (0, 0)), row(LANES)],
        out_specs=[row(ql), row(kvl), row(kvl), row(QK_ROPE_DIM), row(LANES)],
        out_shape=[jax.ShapeDtypeStruct((m, ql), BF16), jax.ShapeDtypeStruct((m, kvl), F32),
                   jax.ShapeDtypeStruct((m, kvl), BF16), jax.ShapeDtypeStruct((m, QK_ROPE_DIM), F32),
                   jax.ShapeDtypeStruct((m, LANES), BF16)],
        compiler_params=_params(("parallel",)),
        name="mla_down",
    )(h, w_cat, g_q, g_kv, tab)


def _q_up_kernel(cq_ref, w_ref, tab_ref, q_ref):
    cq = cq_ref[...]
    tab = tab_ref[...]
    for hd in range(N_HEADS):
        acc = _dot(cq, w_ref[hd])
        q_ref[hd, :, :QK_NOPE_DIM] = acc[:, :QK_NOPE_DIM].astype(BF16)
        q_ref[hd, :, QK_NOPE_DIM:] = _rope_combine(acc[:, QK_NOPE_DIM:] * tab).astype(BF16)


def _q_up(cq, w_q, tab):
    m, ql = cq.shape
    tm = _pick_tile(m, 512)
    hw = 2 * LANES
    return pl.pallas_call(
        _q_up_kernel,
        grid=(m // tm,),
        in_specs=[pl.BlockSpec((tm, ql), lambda i: (i, 0)),
                  pl.BlockSpec((N_HEADS, ql, hw), lambda i: (0, 0, 0)),
                  pl.BlockSpec((tm, LANES), lambda i: (i, 0))],
        out_specs=pl.BlockSpec((N_HEADS, tm, hw), lambda i: (0, i, 0)),
        out_shape=jax.ShapeDtypeStruct((N_HEADS, m, hw), BF16),
        compiler_params=_params(("parallel",)),
        name="q_up",
    )(cq, w_q, tab)


def _kv_up_kernel(ckv_ref, w_ref, krb_ref, k_ref, v_ref):
    ckv = ckv_ref[...]
    krb = krb_ref[...]
    for hd in range(N_HEADS):
        acc = _dot(ckv, w_ref[hd])
        k_ref[hd, :, :QK_NOPE_DIM] = acc[:, :QK_NOPE_DIM].astype(BF16)
        k_ref[hd, :, QK_NOPE_DIM:] = krb
        v_ref[hd] = acc[:, QK_NOPE_DIM:].astype(BF16)


def _kv_up(ckv_b, w_kv, krb):
    t, kvl = ckv_b.shape
    tm = _pick_tile(t, 512)
    hw = 2 * LANES
    return pl.pallas_call(
        _kv_up_kernel,
        grid=(t // tm,),
        in_specs=[pl.BlockSpec((tm, kvl), lambda i: (i, 0)),
                  pl.BlockSpec((N_HEADS, kvl, hw), lambda i: (0, 0, 0)),
                  pl.BlockSpec((tm, LANES), lambda i: (i, 0))],
        out_specs=[pl.BlockSpec((N_HEADS, tm, hw), lambda i: (0, i, 0)),
                   pl.BlockSpec((N_HEADS, tm, V_HEAD_DIM), lambda i: (0, i, 0))],
        out_shape=[jax.ShapeDtypeStruct((N_HEADS, t, hw), BF16), jax.ShapeDtypeStruct((N_HEADS, t, V_HEAD_DIM), BF16)],
        compiler_params=_params(("parallel",)),
        name="kv_up",
    )(ckv_b, w_kv, krb)


FLASH_HEADS = 8


def _flash_kernel(il_ref, jl_ref, q_ref, k_ref, v_ref, o_ref, m_sc, l_sc, acc_sc):
    p = pl.program_id(1)
    i = il_ref[p]
    j = jl_ref[p]

    @pl.when(j == 0)
    def _():
        m_sc[...] = jnp.full(m_sc.shape, NEG_INF, F32)
        l_sc[...] = jnp.zeros(l_sc.shape, F32)
        acc_sc[...] = jnp.zeros(acc_sc.shape, F32)

    def update(masked):
        for hh in range(FLASH_HEADS):
            t = lax.dot_general(k_ref[hh], q_ref[hh], (((1,), (1,)), ((), ())), preferred_element_type=F32) * SCORE_LOG2_SCALE
            if masked:
                key = lax.broadcasted_iota(jnp.int32, t.shape, 0)
                qry = lax.broadcasted_iota(jnp.int32, t.shape, 1)
                t = jnp.where(key <= qry, t, NEG_INF)
            m_prev = m_sc[hh]
            m_new = jnp.maximum(m_prev, jnp.max(t, axis=0, keepdims=True))
            corr = jnp.exp2(m_prev - m_new)
            pr = jnp.exp2(t - m_new)
            l_sc[hh] = l_sc[hh] * corr + jnp.sum(pr, axis=0, keepdims=True)
            pv = lax.dot_general(v_ref[hh], pr.astype(BF16), (((0,), (0,)), ((), ())), preferred_element_type=F32)
            acc_sc[hh] = acc_sc[hh] * corr + pv
            m_sc[hh] = m_new

    @pl.when(j < i)
    def _():
        update(False)

    @pl.when(j == i)
    def _():
        update(True)
        for hh in range(FLASH_HEADS):
            o_ref[:, hh * V_HEAD_DIM:(hh + 1) * V_HEAD_DIM] = jnp.transpose(acc_sc[hh] / l_sc[hh]).astype(o_ref.dtype)


def _flash_attention(q, k, v, t):
    tq = _pick_tile(t, 1024)
    nq = t // tq
    pairs = [(i, j) for i in range(nq) for j in range(i + 1)]
    il = jnp.asarray([p[0] for p in pairs], jnp.int32)
    jl = jnp.asarray([p[1] for p in pairs], jnp.int32)
    hw = q.shape[2]
    hg = FLASH_HEADS
    grid_spec = pltpu.PrefetchScalarGridSpec(
        num_scalar_prefetch=2,
        grid=(N_HEADS // hg, len(pairs)),
        in_specs=[pl.BlockSpec((hg, tq, hw), lambda h, p, il, jl: (h, il[p], 0)),
                  pl.BlockSpec((hg, tq, hw), lambda h, p, il, jl: (h, jl[p], 0)),
                  pl.BlockSpec((hg, tq, V_HEAD_DIM), lambda h, p, il, jl: (h, jl[p], 0))],
        out_specs=pl.BlockSpec((tq, hg * V_HEAD_DIM), lambda h, p, il, jl: (il[p], h)),
        scratch_shapes=[pltpu.VMEM((hg, 1, tq), F32), pltpu.VMEM((hg, 1, tq), F32), pltpu.VMEM((hg, V_HEAD_DIM, tq), F32)],
    )
    return pl.pallas_call(
        _flash_kernel,
        grid_spec=grid_spec,
        out_shape=jax.ShapeDtypeStruct((t, N_HEADS * V_HEAD_DIM), BF16),
        compiler_params=_params(("parallel", "arbitrary")),
        name="flash_attention",
    )(il, jl, q, k, v)


def _bmm_kernel(a_ref, w_ref, o_ref):
    o_ref[...] = _dot(a_ref[...], w_ref[...]).astype(o_ref.dtype)


def _bmm(a, w, out_dtype):
    nh, m, kd = a.shape
    n = w.shape[2]
    return pl.pallas_call(
        _bmm_kernel,
        grid=(nh,),
        in_specs=[pl.BlockSpec((None, m, kd), lambda h: (h, 0, 0)), pl.BlockSpec((None, kd, n), lambda h: (h, 0, 0))],
        out_specs=pl.BlockSpec((None, m, n), lambda h: (h, 0, 0)),
        out_shape=jax.ShapeDtypeStruct((nh, m, n), out_dtype),
        compiler_params=_params(("parallel",)),
        name="head_matmul",
    )(a, w)


def _decode_kernel(pt_ref, qcat_ref, new_ref, acc0_ref, ck_hbm, krt_hbm, o_ref, ckbuf, krbuf, sem,
                   m_sc, l_sc, acc_sc, sa_sc, sb_sc, ca_sc, cb_sc, *, hp, ps, kvl, layer):
    pp = 2 * hp
    n_steps = pl.num_programs(1)
    p = pl.program_id(1)
    step = pl.program_id(0) * n_steps + p
    last_step = pl.num_programs(0) * n_steps - 1
    slot = step % 2
    qlat = qcat_ref[:kvl, :]
    qrope = qcat_ref[kvl:, :]

    def page_copies(st, sl, jj):
        page = pt_ref[st * pp + jj]
        return (pltpu.make_async_copy(ck_hbm.at[layer, page], ckbuf.at[sl, jj], sem.at[sl, 0]),
                pltpu.make_async_copy(krt_hbm.at[layer, page], krbuf.at[sl, jj], sem.at[sl, 1]))

    @pl.when(step == 0)
    def _():
        for jj in range(pp):
            for cp in page_copies(0, 0, jj):
                cp.start()

    for jj in range(pp):
        for cp in page_copies(step, slot, jj):
            cp.wait()

    @pl.when(p == 0)
    def _():
        s0 = (_dot(new_ref[:, :kvl], qlat) + _dot(new_ref[:, kvl:], qrope)) * SCORE_LOG2_SCALE
        m_sc[...] = s0[0:1]
        l_sc[...] = jnp.ones(l_sc.shape, F32)
        acc_sc[...] = acc0_ref[...]
        sb_sc[...] = jnp.full(sb_sc.shape, NEG_INF, F32)
        cb_sc[...] = jnp.zeros(cb_sc.shape, BF16)

    nxt = jnp.minimum(step + 1, last_step)

    def scores(first, s_sc, c_sc):
        for jj in range(first, first + hp):
            for cp in page_copies(nxt, 1 - slot, jj):
                cp.start()
            rows = slice((jj - first) * ps, (jj - first + 1) * ps)
            ckb = ckbuf[slot, jj].astype(BF16)
            c_sc[rows, :] = ckb
            s_rope = lax.dot_general(krbuf[slot, jj].astype(BF16), qrope, (((0,), (0,)), ((), ())),
                                     preferred_element_type=F32)
            s_sc[rows, :] = (_dot(ckb, qlat) + s_rope) * SCORE_LOG2_SCALE

    def update(s_sc, c_sc):
        t = s_sc[...]
        m_prev = m_sc[...]
        m_new = jnp.maximum(m_prev, jnp.max(t, axis=0, keepdims=True))
        corr = jnp.exp2(m_prev - m_new)
        pr = jnp.exp2(t - m_new)
        l_sc[...] = l_sc[...] * corr + jnp.sum(pr, axis=0, keepdims=True)
        pv = lax.dot_general(c_sc[...], pr.astype(BF16), (((0,), (0,)), ((), ())), preferred_element_type=F32)
        acc_sc[...] = acc_sc[...] * corr + pv
        m_sc[...] = m_new

    scores(0, sa_sc, ca_sc)
    update(sb_sc, cb_sc)
    scores(hp, sb_sc, cb_sc)
    update(sa_sc, ca_sc)

    @pl.when(p == n_steps - 1)
    def _():
        update(sb_sc, cb_sc)
        o_ref[...] = acc_sc[...] / l_sc[...]

    @pl.when(step == last_step)
    def _():
        for jj in range(pp):
            for cp in page_copies(nxt, 1 - slot, jj):
                cp.wait()


def _decode_attention(page_table, q_cat_t, new_kv, acc0_t, cache_ckv, cache_krope_t, layer):
    bsz, n_pages = page_table.shape
    ps, kvl = cache_ckv.shape[2], cache_ckv.shape[3]
    rd = cache_krope_t.shape[2]
    pp = math.gcd(n_pages, 32)
    assert pp % 2 == 0
    pt_flat = page_table.reshape(-1)
    seq = lambda rows, width: pl.BlockSpec((None, rows, width), lambda b, p, pt: (b, 0, 0))
    hbm = pl.BlockSpec(memory_space=pl.ANY)
    half_rows = pp // 2 * ps
    grid_spec = pltpu.PrefetchScalarGridSpec(
        num_scalar_prefetch=1,
        grid=(bsz, n_pages // pp),
        in_specs=[seq(kvl + rd, N_HEADS), seq(BF16_SUBLANES, kvl + rd), seq(kvl, N_HEADS), hbm, hbm],
        out_specs=seq(kvl, N_HEADS),
        scratch_shapes=[pltpu.VMEM((2, pp, ps, kvl), F32), pltpu.VMEM((2, pp, rd, ps), F32),
                        pltpu.SemaphoreType.DMA((2, 2)),
                        pltpu.VMEM((1, N_HEADS), F32), pltpu.VMEM((1, N_HEADS), F32), pltpu.VMEM((kvl, N_HEADS), F32),
                        pltpu.VMEM((half_rows, N_HEADS), F32), pltpu.VMEM((half_rows, N_HEADS), F32),
                        pltpu.VMEM((half_rows, kvl), BF16), pltpu.VMEM((half_rows, kvl), BF16)],
    )
    return pl.pallas_call(
        functools.partial(_decode_kernel, hp=pp // 2, ps=ps, kvl=kvl, layer=layer),
        grid_spec=grid_spec,
        out_shape=jax.ShapeDtypeStruct((bsz, kvl, N_HEADS), F32),
        compiler_params=_params(("arbitrary", "arbitrary")),
        name="decode_attention",
    )(pt_flat, q_cat_t, new_kv, acc0_t, cache_ckv, cache_krope_t)


def _proj_res_kernel(a_ref, w_ref, x_ref, ga_ref, o_ref):
    o_ref[...] = x_ref[...] + ga_ref[...] * _dot(a_ref[...], w_ref[...])


def _proj_res(a, w, x, gate):
    m, kd = a.shape
    d = w.shape[1]
    tm = _pick_tile(m, 512)
    return pl.pallas_call(
        _proj_res_kernel,
        grid=(m // tm,),
        in_specs=[pl.BlockSpec((tm, kd), lambda i: (i, 0)), pl.BlockSpec((kd, d), lambda i: (0, 0)),
                  pl.BlockSpec((tm, d), lambda i: (i, 0)), _row_spec(gate, tm)],
        out_specs=pl.BlockSpec((tm, d), lambda i: (i, 0)),
        out_shape=jax.ShapeDtypeStruct((m, d), F32),
        compiler_params=_params(("parallel",)),
        name="proj_residual",
    )(a, w, x, gate)


CONV_HALO = 8


def _conv_prompt_kernel(h_ref, wb_ref, wc_ref, wx_ref, wcv_ref, st_ref, z_ref, ns_ref, ubuf, *, tm):
    i = pl.program_id(1)

    @pl.when(i == 0)
    def _():
        ubuf[0:CONV_HALO, :] = st_ref[...]

    @pl.when(i > 0)
    def _():
        ubuf[0:CONV_HALO, :] = ubuf[tm:tm + CONV_HALO, :]

    h = h_ref[...]
    b_gate = _dot(h, wb_ref[...])
    ubuf[CONV_HALO:, :] = _dot(h, wc_ref[...]) * _dot(h, wx_ref[...])
    wcv = wcv_ref[...]
    conv = (wcv[0:1] * ubuf[pl.ds(CONV_HALO - 2, tm), :] + wcv[1:2] * ubuf[pl.ds(CONV_HALO - 1, tm), :]
            + wcv[2:3] * ubuf[pl.ds(CONV_HALO, tm), :])
    z_ref[...] = (b_gate * conv).astype(z_ref.dtype)

    @pl.when(i == pl.num_programs(1) - 1)
    def _():
        ns_ref[...] = ubuf[tm:tm + CONV_HALO, :]


def _conv_prompt(h, w_in, w_conv, state8):
    t, d = h.shape
    tm = _pick_tile(t, 512)
    tn = 512
    nt = d // tn
    w_spec = lambda part: pl.BlockSpec((d, tn), lambda j, i: (0, j + part * nt))
    return pl.pallas_call(
        functools.partial(_conv_prompt_kernel, tm=tm),
        grid=(nt, t // tm),
        in_specs=[pl.BlockSpec((tm, d), lambda j, i: (i, 0)), w_spec(0), w_spec(1), w_spec(2),
                  pl.BlockSpec((w_conv.shape[0], tn), lambda j, i: (0, j)),
                  pl.BlockSpec((CONV_HALO, tn), lambda j, i: (0, j))],
        out_specs=[pl.BlockSpec((tm, tn), lambda j, i: (i, j)), pl.BlockSpec((CONV_HALO, tn), lambda j, i: (0, j))],
        out_shape=[jax.ShapeDtypeStruct((t, d), BF16), jax.ShapeDtypeStruct((CONV_HALO, d), F32)],
        scratch_shapes=[pltpu.VMEM((tm + CONV_HALO, tn), F32)],
        compiler_params=_params(("parallel", "arbitrary")),
        name="conv_prompt",
    )(h, w_in, w_in, w_in, w_conv, state8)


def _conv_sample_kernel(h_ref, wb_ref, wc_ref, wx_ref, wcv_ref, s0_ref, s1_ref, z_ref, u_ref):
    h = h_ref[...]
    b_gate = _dot(h, wb_ref[...])
    u = _dot(h, wc_ref[...]) * _dot(h, wx_ref[...])
    wcv = wcv_ref[...]
    conv = wcv[0:1] * s0_ref[...] + wcv[1:2] * s1_ref[...] + wcv[2:3] * u
    z_ref[...] = (b_gate * conv).astype(z_ref.dtype)
    u_ref[...] = u


def _conv_sample(h, w_in, w_conv, s0, s1):
    b, d = h.shape
    tn = 512
    nt = d // tn
    w_spec = lambda part: pl.BlockSpec((d, tn), lambda j: (0, j + part * nt))
    col = pl.BlockSpec((b, tn), lambda j: (0, j))
    return pl.pallas_call(
        _conv_sample_kernel,
        grid=(nt,),
        in_specs=[pl.BlockSpec((b, d), lambda j: (0, 0)), w_spec(0), w_spec(1), w_spec(2),
                  pl.BlockSpec((w_conv.shape[0], tn), lambda j: (0, j)), col, col],
        out_specs=[col, col],
        out_shape=[jax.ShapeDtypeStruct((b, d), BF16), jax.ShapeDtypeStruct((b, d), F32)],
        compiler_params=_params(("parallel",)),
        name="conv_sample",
    )(h, w_in, w_in, w_in, w_conv, s0, s1)


MOE_BLOCK = 512
MOE_TN = 1024
MOE_TN_DOWN = 2048
MXU_COLS = 256


def _col_chunks(tn):
    return [slice(c, c + MXU_COLS) for c in range(0, tn, MXU_COLS)]


def _expert_weight_stream(w_hbm, wbuf, sem, layer, col_starts, tn, be_ref, new_ref, slot_ref, nxt_ref, b):
    def copies(e, slot):
        return [pltpu.make_async_copy(w_hbm.at[layer, e, :, pl.ds(pl.multiple_of(c0, tn), tn)],
                                      wbuf.at[slot, k], sem.at[slot, k])
                for k, c0 in enumerate(col_starts)]

    slot = slot_ref[b]

    @pl.when(new_ref[b] == 1)
    def _():
        e = be_ref[b]

        @pl.when(b == 0)
        def _():
            for cp in copies(e, slot):
                cp.start()

        for cp in copies(e, slot):
            cp.wait()
        nxt = nxt_ref[b]

        @pl.when(nxt >= 0)
        def _():
            for cp in copies(nxt, 1 - slot):
                cp.start()

    return slot


def _moe_gu_kernel(be_ref, nu_ref, new_ref, slot_ref, nxt_ref, xs_ref, w_hbm, bg_ref, bu_ref, a_ref, wbuf, sem, *, layer):
    j = pl.program_id(0)
    b = pl.program_id(1)
    tn = a_ref.shape[1]
    n_tiles = pl.num_programs(0)

    @pl.when(b < nu_ref[0])
    def _():
        slot = _expert_weight_stream(w_hbm, wbuf, sem, layer, (j * tn, (j + n_tiles) * tn), tn,
                                     be_ref, new_ref, slot_ref, nxt_ref, b)
        x = xs_ref[...].astype(BF16)
        for cols in _col_chunks(tn):
            g = jnp.minimum(_dot(x, wbuf[slot, 0, :, cols].astype(BF16)) + bg_ref[:, cols], SWIGLU_LIMIT)
            u = jnp.clip(_dot(x, wbuf[slot, 1, :, cols].astype(BF16)) + bu_ref[:, cols], -SWIGLU_LIMIT, SWIGLU_LIMIT)
            a_ref[:, cols] = ((u + 1.0) * (g * jax.nn.sigmoid(SWIGLU_ALPHA * g))).astype(a_ref.dtype)


def _moe_down_kernel(be_ref, nu_ref, new_ref, slot_ref, nxt_ref, a_ref, w_hbm, b_ref, y_ref, wbuf, sem, *, layer):
    j = pl.program_id(0)
    b = pl.program_id(1)
    tn = y_ref.shape[1]

    @pl.when(b < nu_ref[0])
    def _():
        slot = _expert_weight_stream(w_hbm, wbuf, sem, layer, (j * tn,), tn, be_ref, new_ref, slot_ref, nxt_ref, b)
        a = a_ref[...]
        for cols in _col_chunks(tn):
            y_ref[:, cols] = _dot(a, wbuf[slot, 0, :, cols].astype(BF16)) + b_ref[:, cols]


def _moe_experts(xs, blk_expert, n_used, blk_new, blk_slot, blk_next, w_gu, b_gu, w_down, b_down, layer):
    r, d = xs.shape
    de = w_down.shape[2]
    nb = r // MOE_BLOCK
    tn = MOE_TN
    ntg = de // tn
    blk = lambda b, nu: jnp.minimum(b, nu[0] - 1)
    prefetch = (blk_expert, n_used, blk_new, blk_slot, blk_next)
    hbm = pl.BlockSpec(memory_space=pl.ANY)

    act = pl.pallas_call(
        functools.partial(_moe_gu_kernel, layer=layer),
        grid_spec=pltpu.PrefetchScalarGridSpec(
            num_scalar_prefetch=len(prefetch),
            grid=(ntg, nb),
            in_specs=[pl.BlockSpec((MOE_BLOCK, d), lambda j, b, be, nu, *_: (blk(b, nu), 0)),
                      hbm,
                      pl.BlockSpec((None, None, 1, tn), lambda j, b, be, nu, *_: (layer, be[blk(b, nu)], 0, j)),
                      pl.BlockSpec((None, None, 1, tn), lambda j, b, be, nu, *_: (layer, be[blk(b, nu)], 0, j + ntg))],
            out_specs=pl.BlockSpec((MOE_BLOCK, tn), lambda j, b, be, nu, *_: (blk(b, nu), j)),
            scratch_shapes=[pltpu.VMEM((2, 2, d, tn), F32), pltpu.SemaphoreType.DMA((2, 2))],
        ),
        out_shape=jax.ShapeDtypeStruct((r, de), BF16),
        compiler_params=_params(("arbitrary", "arbitrary"), 56),
        name="moe_gate_up",
    )(*prefetch, xs, w_gu, b_gu, b_gu)

    tn = MOE_TN_DOWN
    ntd = d // tn
    return pl.pallas_call(
        functools.partial(_moe_down_kernel, layer=layer),
        grid_spec=pltpu.PrefetchScalarGridSpec(
            num_scalar_prefetch=len(prefetch),
            grid=(ntd, nb),
            in_specs=[pl.BlockSpec((MOE_BLOCK, de), lambda j, b, be, nu, *_: (blk(b, nu), 0)),
                      hbm,
                      pl.BlockSpec((None, None, 1, tn), lambda j, b, be, nu, *_: (layer, be[blk(b, nu)], 0, j))],
            out_specs=pl.BlockSpec((MOE_BLOCK, tn), lambda j, b, be, nu, *_: (blk(b, nu), j)),
            scratch_shapes=[pltpu.VMEM((2, 1, de, tn), F32), pltpu.SemaphoreType.DMA((2, 1))],
        ),
        out_shape=jax.ShapeDtypeStruct((r, d), F32),
        compiler_params=_params(("arbitrary", "arbitrary"), 56),
        name="moe_down",
    )(*prefetch, act, w_down, b_down)


def _moe(tokens, logits, w_gu, b_gu, w_down, b_down, layer):
    n, d = tokens.shape
    ne = logits.shape[1]
    top_val, top_idx = lax.top_k(logits, TOP_K)
    gates = jax.nn.softmax(top_val, axis=-1)
    n_assign = n * TOP_K
    flat_e = top_idx.reshape(-1).astype(jnp.int32)
    order = jnp.argsort(flat_e, stable=True).astype(jnp.int32)
    rank = jnp.argsort(order).astype(jnp.int32)
    counts = jnp.sum((flat_e[:, None] == jnp.arange(ne, dtype=jnp.int32)[None, :]).astype(jnp.int32), axis=0)
    padded = (counts + MOE_BLOCK - 1) // MOE_BLOCK * MOE_BLOCK
    pad_end = jnp.cumsum(padded)
    pad_start = pad_end - padded
    start = jnp.cumsum(counts) - counts
    pos = (rank + (pad_start - start)[flat_e]).reshape(n, TOP_K)
    n_blocks = pl.cdiv(n_assign, MOE_BLOCK) + ne
    n_rows = n_blocks * MOE_BLOCK
    blk_row0 = jnp.arange(n_blocks, dtype=jnp.int32) * MOE_BLOCK
    blk_expert = jnp.minimum(jnp.sum((pad_end[None, :] <= blk_row0[:, None]).astype(jnp.int32), axis=1), ne - 1)
    n_used = (pad_end[-1] // MOE_BLOCK).astype(jnp.int32).reshape(1)
    blk_new = jnp.concatenate([jnp.ones((1,), jnp.int32), (blk_expert[1:] != blk_expert[:-1]).astype(jnp.int32)])
    blk_slot = (jnp.cumsum(blk_new) - 1) % 2
    experts = jnp.arange(ne, dtype=jnp.int32)
    first_at_or_after = lax.cummin(jnp.where(counts > 0, experts, ne), axis=0, reverse=True)
    next_expert = jnp.concatenate([first_at_or_after[1:], jnp.full((1,), ne, jnp.int32)])
    blk_next = jnp.where(next_expert < ne, next_expert, -1)[blk_expert]
    off = (blk_row0 - pad_start[blk_expert])[:, None] + jnp.arange(MOE_BLOCK, dtype=jnp.int32)[None, :]
    src = jnp.clip(start[blk_expert][:, None] + off, 0, n_assign - 1)
    row_tok = jnp.where(off < counts[blk_expert][:, None], order[src] // TOP_K, 0).reshape(n_rows)

    xs = tokens[row_tok]
    n_layers, _, de, _ = w_down.shape
    y = _moe_experts(xs, blk_expert, n_used, blk_new, blk_slot.astype(jnp.int32), blk_next.astype(jnp.int32),
                     w_gu, b_gu.reshape(n_layers, ne, 1, 2 * de), w_down, b_down.reshape(n_layers, ne, 1, d), layer)
    return y[pos.T.reshape(-1)].reshape(TOP_K, n, d), gates


def _rope_table(pos):
    half = QK_ROPE_DIM // 2
    with jax.ensure_compile_time_eval(), jax.default_device(jax.devices("cpu")[0]):
        inv = jnp.exp(-math.log(ROPE_THETA) * jnp.arange(half, dtype=F32) / half)
        ang = jnp.asarray(pos, jnp.int32).astype(F32)[:, None] * inv[None, :]
        cos, sin = jnp.cos(ang), jnp.sin(ang)
        return jnp.concatenate([cos, cos, sin, sin], axis=-1)


def _rot_cols(w):
    half = QK_ROPE_DIM // 2
    return jnp.concatenate([-w[..., half:], w[..., :half]], axis=-1)


def kernel(x_prompt, x_sample, c_prompt, c_sample, cache_ckv, cache_krope, state_conv, page_table, w_ada, b_ada, g_mix_norm, g_ffn_norm, w_dq, g_q, w_uq, w_dkv, g_kv, w_ukv, w_o_mla, w_conv_in, w_conv, w_conv_out, w_router, b_router, w_gu, b_gu, w_down, b_down, g_final):
    bp, tp, d = x_prompt.shape
    bs, ts, _ = x_sample.shape
    assert bp == 1 and ts == 1
    depth = w_ada.shape[0]
    ne = w_router.shape[2]
    ql = w_dq.shape[2]
    kvl = g_kv.shape[1]
    past_len = page_table.shape[1] * cache_ckv.shape[2]

    xp = x_prompt.reshape(tp, d)
    xs = x_sample.reshape(bs, d)

    n_c = 1 + bs
    mc = pl.cdiv(n_c, BF16_SUBLANES) * BF16_SUBLANES
    c_all = jnp.concatenate([c_prompt, c_sample, jnp.zeros((mc - n_c, d), F32)], axis=0)
    mod = _adaln(c_all, w_ada, b_ada)

    tab_p = _rope_table(list(range(tp)))
    tab_s = _rope_table([past_len] * bs)

    ckv_p, kr_p, ckv_s, kr_s, conv_p, conv_s = [], [], [], [], [], []
    hp = hs = None
    for layer in range(depth):
        mp = [mod[layer, 0:1, k * d:(k + 1) * d] for k in range(N_ADA)]
        ms = [mod[layer, 1:1 + bs, k * d:(k + 1) * d] for k in range(N_ADA)]
        g_mix = g_mix_norm[layer].reshape(1, d)
        g_ffn = g_ffn_norm[layer].reshape(1, d)
        if layer == 0:
            hp = _mix_norm(xp, g_mix, mp[0], mp[1])
            hs = _mix_norm(xs, g_mix, ms[0], ms[1])
        j = layer // 2
        if layer % 2 == 0:
            w_rope = w_dkv[j][:, kvl:]
            w_cat = jnp.concatenate([w_dq[j], w_dkv[j][:, :kvl], w_rope, _rot_cols(w_rope)], axis=1).astype(BF16)
            wq = w_uq[j].reshape(ql, N_HEADS, QK_NOPE_DIM + QK_ROPE_DIM)
            wq_rope = wq[..., QK_NOPE_DIM:]
            wq = jnp.concatenate([wq[..., :QK_NOPE_DIM], wq_rope, _rot_cols(wq_rope)], axis=-1)
            wq = jnp.transpose(wq, (1, 0, 2)).astype(BF16)
            wkv3 = w_ukv[j].reshape(kvl, N_HEADS, QK_NOPE_DIM + V_HEAD_DIM)
            wkv = jnp.transpose(wkv3, (1, 0, 2)).astype(BF16)
            w_uk_t = jnp.transpose(wkv3[..., :QK_NOPE_DIM], (1, 2, 0)).astype(BF16)
            w_uv = jnp.transpose(wkv3[..., QK_NOPE_DIM:], (1, 0, 2)).astype(BF16)
            w_o = w_o_mla[j].astype(BF16)
            gq = g_q[j].reshape(1, ql)
            gkv = g_kv[j].reshape(1, kvl)

            cq, ckv, ckv_b, kr, kr_b = _mla_down(hp, w_cat, gq, gkv, tab_p, ql, kvl)
            q = _q_up(cq, wq, tab_p)
            k, v = _kv_up(ckv_b, wkv, kr_b)
            o = _flash_attention(q, k, v, tp)
            xp = _proj_res(o, w_o, xp, mp[2])
            ckv_p.append(ckv.reshape(bp, tp, kvl))
            kr_p.append(kr.reshape(bp, tp, QK_ROPE_DIM))

            cq_s, ckv_sn, ckv_sb, kr_sn, kr_sb = _mla_down(hs, w_cat, gq, gkv, tab_s, ql, kvl)
            q_s = _q_up(cq_s, wq, tab_s)
            q_lat = _bmm(q_s[:, :, :QK_NOPE_DIM], w_uk_t, BF16)
            q_cat = jnp.concatenate([q_lat, q_s[:, :, QK_NOPE_DIM:QK_NOPE_DIM + QK_ROPE_DIM]], axis=2)
            q_cat_t = jnp.transpose(q_cat, (1, 2, 0))
            new_kv = jnp.concatenate([ckv_sb, kr_sb[:, :QK_ROPE_DIM]], axis=1)
            new_kv = jnp.broadcast_to(new_kv[:, None, :], (bs, BF16_SUBLANES, kvl + QK_ROPE_DIM))
            acc0_t = jnp.broadcast_to(ckv_sb.astype(F32)[:, :, None], (bs, kvl, N_HEADS))
            o_lat_t = _decode_attention(page_table, q_cat_t, new_kv, acc0_t, cache_ckv,
                                        jnp.swapaxes(cache_krope, 2, 3), j)
            o_lat = jnp.transpose(o_lat_t, (2, 0, 1)).astype(BF16)
            o_s = _bmm(o_lat, w_uv, BF16)
            o_s = jnp.transpose(o_s, (1, 0, 2)).reshape(bs, N_HEADS * V_HEAD_DIM)
            xs = _proj_res(o_s, w_o, xs, ms[2])
            ckv_s.append(ckv_sn.reshape(bs, ts, kvl))
            kr_s.append(kr_sn.reshape(bs, ts, QK_ROPE_DIM))
        else:
            w_in = w_conv_in[j].astype(BF16)
            w_out = w_conv_out[j].astype(BF16)
            zero_state = jnp.zeros((CONV_HALO, d), F32)
            z_p, ns_p = _conv_prompt(hp, w_in, w_conv[j], zero_state)
            xp = _proj_res(z_p, w_out, xp, mp[2])
            conv_p.append(ns_p[CONV_HALO - (w_conv.shape[1] - 1):].reshape(bp, w_conv.shape[1] - 1, d))
            s0, s1 = state_conv[j][:, 0, :], state_conv[j][:, 1, :]
            z_s, u_s = _conv_sample(hs, w_in, w_conv[j], s0, s1)
            xs = _proj_res(z_s, w_out, xs, ms[2])
            conv_s.append(jnp.stack([s1, u_s], axis=1))

        wr = jnp.pad(w_router[layer], ((0, 0), (0, LANES - ne)))
        wr_hi = wr.astype(BF16)
        wr_lo = (wr - wr_hi.astype(F32)).astype(BF16)
        br = jnp.pad(b_router[layer], (0, LANES - ne)).reshape(1, LANES)
        h2p, lg_p = _ffn_norm(xp, g_ffn, mp[3], mp[4], wr_hi, wr_lo, br)
        h2s, lg_s = _ffn_norm(xs, g_ffn, ms[3], ms[4], wr_hi, wr_lo, br)
        tokens = jnp.concatenate([h2p, h2s], axis=0)
        logits = jnp.concatenate([lg_p, lg_s], axis=0)[:, :ne]
        fs, route_gates = _moe(tokens, logits, w_gu, b_gu, w_down, b_down, layer)
        if layer + 1 < depth:
            g_next = g_mix_norm[layer + 1].reshape(1, d)
            mpn = [mod[layer + 1, 0:1, k * d:(k + 1) * d] for k in range(2)]
            msn = [mod[layer + 1, 1:1 + bs, k * d:(k + 1) * d] for k in range(2)]
            xp, hp = _res_norm(xp, fs, route_gates, mp[5], g_next, mpn[0], mpn[1], 0)
            xs, hs = _res_norm(xs, fs, route_gates, ms[5], g_next, msn[0], msn[1], tp)
        else:
            g_fin = g_final.reshape(1, d)
            y_prompt = _res_final(xp, fs, route_gates, mp[5], g_fin, 0).reshape(bp, tp, d)
            y_sample = _res_final(xs, fs, route_gates, ms[5], g_fin, tp).reshape(bs, ts, d)

    return (y_prompt, y_sample, jnp.stack(ckv_p), jnp.stack(kr_p), jnp.stack(ckv_s), jnp.stack(kr_s),
            jnp.stack(conv_p), jnp.stack(conv_s))
```
